```python
import math
import jax, jax.numpy as jnp
from jax import lax
import numpy as np

D_MODEL = 4096
BATCH = 4
SEQ = 4096
DEPTH = 2

N_A_LAYERS = DEPTH // 2
N_B_LAYERS = DEPTH - N_A_LAYERS
CONV_WIDTH = D_MODEL
CONV_KERNEL = 31
MLA_HEADS = 64
Q_LORA_RANK = 1024
KV_LORA_RANK = 512
QK_NOPE_DIM = 128
QK_ROPE_DIM = 64
V_HEAD_DIM = 128
MLA_WIDTH = MLA_HEADS * V_HEAD_DIM
ROPE_BASE = 10000.0
Q_BLOCK = 128
LN_EPS = 1e-5
RMS_EPS = 1e-6
DEEPNORM_ALPHA = (2.0 * DEPTH) ** 0.25
DEEPNORM_BETA = (8.0 * DEPTH) ** -0.25

kernel_name = "yoco_conformer_mla_deepnorm_adaln"


def _layernorm(x, g=None, b=None):
    xf = x.astype(jnp.float32)
    mu = jnp.mean(xf, axis=-1, keepdims=True)
    var = jnp.mean(jnp.square(xf - mu), axis=-1, keepdims=True)
    y = (xf - mu) * lax.rsqrt(var + LN_EPS)
    if g is not None:
        y = y * g.astype(jnp.float32) + b.astype(jnp.float32)
    return y.astype(x.dtype)


def _rmsnorm(x, g):
    xf = x.astype(jnp.float32)
    y = xf * lax.rsqrt(jnp.mean(jnp.square(xf), axis=-1, keepdims=True) + RMS_EPS)
    return (y * g.astype(jnp.float32)).astype(x.dtype)


def _rope(x, pos):
    half = QK_ROPE_DIM // 2
    inv_freq = ROPE_BASE ** (-jnp.arange(half, dtype=jnp.float32) / half)
    ang = pos.astype(jnp.float32)[:, None] * inv_freq[None, :]
    cos = jnp.cos(ang)[None, :, None, :]
    sin = jnp.sin(ang)[None, :, None, :]
    xf = x.astype(jnp.float32)
    x1, x2 = xf[..., :half], xf[..., half:]
    return jnp.concatenate([x1 * cos - x2 * sin, x1 * sin + x2 * cos], axis=-1).astype(x.dtype)


def _modulate(x, c, w_ada, b_ada):
    mod = jax.nn.silu(c) @ w_ada + b_ada
    shift, scale, gate = jnp.split(mod, 3, axis=-1)
    h = _layernorm(x) * (1.0 + scale[:, None, :]) + shift[:, None, :]
    return h, (1.0 + gate)[:, None, :]


def _conformer_conv(h, w_in, w_dw, b_dw, g_cn, b_cn, w_out):
    u = h @ w_in
    a, g, z = jnp.split(u, 3, axis=-1)
    v = a * jax.nn.sigmoid(g)
    v = lax.conv_general_dilated(
        v, w_dw[:, None, :], window_strides=(1,),
        padding=[(CONV_KERNEL - 1, 0)],
        dimension_numbers=("NWC", "WIO", "NWC"),
        feature_group_count=CONV_WIDTH) + b_dw
    v = jax.nn.silu(_layernorm(v, g_cn, b_cn))
    return (v * jax.nn.silu(z)) @ w_out


def _shared_kv(xs, w_kva, g_kv, w_kvb, pos):
    B, S, _ = xs.shape
    kva = xs @ w_kva
    c_kv = _rmsnorm(kva[..., :KV_LORA_RANK], g_kv)
    k_rope = _rope(kva[..., KV_LORA_RANK:][:, :, None, :], pos)[:, :, 0]
    kv = (c_kv @ w_kvb).reshape(B, S, MLA_HEADS, QK_NOPE_DIM + V_HEAD_DIM)
    return kv[..., :QK_NOPE_DIM], k_rope, kv[..., QK_NOPE_DIM:]


def _mla(h, k_nope, k_rope, v, w_in, g_q, w_qb, w_out, pos):
    B, S, _ = h.shape
    u = h @ w_in
    c_q, z = u[..., :Q_LORA_RANK], u[..., Q_LORA_RANK:]
    q = (_rmsnorm(c_q, g_q) @ w_qb).reshape(B, S, MLA_HEADS, QK_NOPE_DIM + QK_ROPE_DIM)
    q_nope = q[..., :QK_NOPE_DIM]
    q_rope = _rope(q[..., QK_NOPE_DIM:], pos)
    scale = (QK_NOPE_DIM + QK_ROPE_DIM) ** -0.5
    outs = []
    for start in range(0, S, Q_BLOCK):
        end = min(start + Q_BLOCK, S)
        s = (jnp.einsum("bqhd,bkhd->bhqk", q_nope[:, start:end], k_nope[:, :end])
             + jnp.einsum("bqhd,bkd->bhqk", q_rope[:, start:end], k_rope[:, :end]))
        s = s.astype(jnp.float32) * scale
        mask = (start + jnp.arange(end - start))[:, None] >= jnp.arange(end)[None, :]
        s = jnp.where(mask[None, None], s, -jnp.inf)
        p = jax.nn.softmax(s, axis=-1).astype(v.dtype)
        outs.append(jnp.einsum("bhqk,bkhd->bqhd", p, v[:, :end]))
    o = jnp.concatenate(outs, axis=1).reshape(B, S, MLA_WIDTH)
    return (o * jax.nn.silu(z)) @ w_out


def setup_inputs(seed: int = 0) -> dict:
    key = jax.random.key(seed)
    ks = jax.random.split(key, 24)
    f32 = jnp.float32
    D = D_MODEL
    nrm = lambda k, shape, s: jax.random.normal(k, shape, f32) * s
    return {
        "x": nrm(ks[0], (BATCH, SEQ, D), 1.0),
        "c": nrm(ks[1], (BATCH, D), 1.0),
        "w_ada": nrm(ks[2], (DEPTH, D, 3 * D), D ** -0.5),
        "b_ada": nrm(ks[3], (DEPTH, 3 * D), 0.02),
        "ln_g": 1.0 + nrm(ks[4], (DEPTH, D), 0.02),
        "ln_b": nrm(ks[5], (DEPTH, D), 0.02),
        "a_w_in": nrm(ks[6], (N_A_LAYERS, D, 3 * CONV_WIDTH), D ** -0.5),
        "a_w_dw": nrm(ks[7], (N_A_LAYERS, CONV_KERNEL, CONV_WIDTH), CONV_KERNEL ** -0.5),
        "a_b_dw": nrm(ks[8], (N_A_LAYERS, CONV_WIDTH), 0.02),
        "a_norm_g": 1.0 + nrm(ks[9], (N_A_LAYERS, CONV_WIDTH), 0.02),
        "a_norm_b": nrm(ks[10], (N_A_LAYERS, CONV_WIDTH), 0.02),
        "a_w_out": nrm(ks[11], (N_A_LAYERS, CONV_WIDTH, D), DEEPNORM_BETA * CONV_WIDTH ** -0.5),
        "b_w_in": nrm(ks[12], (N_B_LAYERS, D, Q_LORA_RANK + MLA_WIDTH), D ** -0.5),
        "b_q_norm_g": 1.0 + nrm(ks[13], (N_B_LAYERS, Q_LORA_RANK), 0.02),
        "b_w_qb": nrm(ks[14], (N_B_LAYERS, Q_LORA_RANK, MLA_HEADS * (QK_NOPE_DIM + QK_ROPE_DIM)), Q_LORA_RANK ** -0.5),
        "b_w_out": nrm(ks[15], (N_B_LAYERS, MLA_WIDTH, D), DEEPNORM_BETA * MLA_WIDTH ** -0.5),
        "kv_w_a": nrm(ks[16], (D, KV_LORA_RANK + QK_ROPE_DIM), D ** -0.5),
        "kv_norm_g": 1.0 + nrm(ks[17], (KV_LORA_RANK,), 0.02),
        "kv_w_b": nrm(ks[18], (KV_LORA_RANK, MLA_HEADS * (QK_NOPE_DIM + V_HEAD_DIM)), KV_LORA_RANK ** -0.5),
    }


def reference(x, c, w_ada, b_ada, ln_g, ln_b, a_w_in, a_w_dw, a_b_dw, a_norm_g, a_norm_b, a_w_out,
              b_w_in, b_q_norm_g, b_w_qb, b_w_out, kv_w_a, kv_norm_g, kv_w_b):
    S = x.shape[1]
    pos = jnp.arange(S, dtype=jnp.int32)
    k_nope = k_rope = v = None
    for layer in range(DEPTH):
        if layer == N_A_LAYERS:
            k_nope, k_rope, v = _shared_kv(x, kv_w_a, kv_norm_g, kv_w_b, pos)
        h, gate = _modulate(x, c, w_ada[layer], b_ada[layer])
        if layer < N_A_LAYERS:
            i = layer
            out = _conformer_conv(h, a_w_in[i], a_w_dw[i], a_b_dw[i], a_norm_g[i], a_norm_b[i], a_w_out[i])
        else:
            j = layer - N_A_LAYERS
            out = _mla(h, k_nope, k_rope, v, b_w_in[j], b_q_norm_g[j], b_w_qb[j], b_w_out[j], pos)
        x = _layernorm(DEEPNORM_ALPHA * x + gate * out, ln_g[layer], ln_b[layer])
    return x
```

```python
import functools
import math

import jax
import jax.numpy as jnp
from jax import lax
from jax.experimental import pallas as pl
from jax.experimental.pallas import tpu as pltpu

QK_NOPE_DIM = 128
QK_ROPE_DIM = 64
V_HEAD_DIM = 128
ROPE_BASE = 10000.0
LN_EPS = 1e-5
RMS_EPS = 1e-6

V7X_LANES = 128
V7X_VMEM_BYTES = 64 * 1024 * 1024
VMEM_LIMIT_BYTES = 56 * 1024 * 1024

F32 = jnp.float32
BF16 = jnp.bfloat16


def _params(*semantics):
    return pltpu.CompilerParams(dimension_semantics=semantics,
                                vmem_limit_bytes=VMEM_LIMIT_BYTES)


def _tile(dim, want):
    t = min(dim, want)
    while dim % t:
        t //= 2
    return t


def _dot(a, b):
    return jnp.dot(a, b, preferred_element_type=F32)


def _silu(x):
    return x * jax.nn.sigmoid(x)


def _ada_kernel(c_ref, w_ref, b_ref, o_ref):
    sc = _silu(c_ref[...]).astype(BF16)
    o_ref[0] = _dot(sc, w_ref[0].astype(BF16)) + b_ref[0]


def _ada_mod(c, w_ada, b_ada):
    L, D, N = w_ada.shape
    B = c.shape[0]
    Bp = max(8, B)
    cp = jnp.pad(c, ((0, Bp - B), (0, 0)))
    tn = _tile(N, 512)
    out = pl.pallas_call(
        _ada_kernel,
        grid=(L, N // tn),
        in_specs=[
            pl.BlockSpec((Bp, D), lambda l, j: (0, 0)),
            pl.BlockSpec((1, D, tn), lambda l, j: (l, 0, j)),
            pl.BlockSpec((1, 1, tn), lambda l, j: (l, 0, j)),
        ],
        out_specs=pl.BlockSpec((1, Bp, tn), lambda l, j: (l, 0, j)),
        out_shape=jax.ShapeDtypeStruct((L, Bp, N), F32),
        compiler_params=_params("parallel", "parallel"),
        name="ada_mod",
    )(cp, w_ada, b_ada.reshape(L, 1, N))
    return out[:, :B].reshape(L, B, 3, D)


def _ln_mod_kernel(x_ref, mod_ref, h_ref):
    x = x_ref[...]
    mu = jnp.mean(x, axis=-1, keepdims=True)
    xc = x - mu
    var = jnp.mean(xc * xc, axis=-1, keepdims=True)
    y = xc * lax.rsqrt(var + LN_EPS)
    shift = mod_ref[0, 0:1, :]
    scale = mod_ref[0, 1:2, :]
    h_ref[...] = (y * (1.0 + scale) + shift).astype(BF16)


def _ln_modulate(x2, mod, S):
    T, D = x2.shape
    tm = _tile(S, 256)
    per_b = S // tm
    return pl.pallas_call(
        _ln_mod_kernel,
        grid=(T // tm,),
        in_specs=[
            pl.BlockSpec((tm, D), lambda i: (i, 0)),
            pl.BlockSpec((1, 3, D), lambda i: (i // per_b, 0, 0)),
        ],
        out_specs=pl.BlockSpec((tm, D), lambda i: (i, 0)),
        out_shape=jax.ShapeDtypeStruct((T, D), BF16),
        compiler_params=_params("parallel"),
        name="ln_modulate",
    )(x2, mod)


def _glu_kernel(h_ref, wa_ref, wg_ref, wz_ref, v_ref, sz_ref):
    h = h_ref[...]
    a = _dot(h, wa_ref[...])
    g = _dot(h, wg_ref[...])
    v_ref[...] = a * jax.nn.sigmoid(g)
    z = _dot(h, wz_ref[...])
    sz_ref[...] = _silu(z)


def _glu_proj(h, w_in):
    T, D = h.shape
    C = w_in.shape[1] // 3
    tm = _tile(T, 1024)
    tn = _tile(C, 256)
    nj = C // tn
    return pl.pallas_call(
        _glu_kernel,
        grid=(T // tm, nj),
        in_specs=[
            pl.BlockSpec((tm, D), lambda i, j: (i, 0)),
            pl.BlockSpec((D, tn), lambda i, j: (0, j)),
            pl.BlockSpec((D, tn), lambda i, j: (0, j + nj)),
            pl.BlockSpec((D, tn), lambda i, j: (0, j + 2 * nj)),
        ],
        out_specs=[
            pl.BlockSpec((tm, tn), lambda i, j: (i, j)),
            pl.BlockSpec((tm, tn), lambda i, j: (i, j)),
        ],
        out_shape=[jax.ShapeDtypeStruct((T, C), F32), jax.ShapeDtypeStruct((T, C), F32)],
        compiler_params=_params("parallel", "arbitrary"),
        name="glu_proj",
    )(h, w_in, w_in, w_in)


CONV_HALO = 32
CONV_ROWS = 64
NORM_ROWS = 16
SUBLANES = 8


def _conv_kernel(vh_ref, vm_ref, sz_ref, w_ref, bdw_ref, g_ref, b_ref, o_ref, slab_ref, cv_ref,
                 *, taps, ts, C):
    s = pl.program_id(1)
    ns = C // V7X_LANES
    lead = CONV_HALO - (taps - 1)
    for sl in range(ns):
        p, par = divmod(sl, 2)
        ls = slice(sl * V7X_LANES, (sl + 1) * V7X_LANES)
        halo = vh_ref[0, :, ls]
        slab_ref[p, pl.ds(par, CONV_HALO, stride=2), :] = jnp.where(s > 0, halo, jnp.zeros_like(halo))
        slab_ref[p, pl.ds(2 * CONV_HALO + par, ts, stride=2), :] = vm_ref[0, :, ls]

    nacc = CONV_ROWS // SUBLANES

    def slab_body(sl, carry):
        p = sl // 2
        par = sl % 2
        wk = [jnp.broadcast_to(w_ref[sl, k:k + 1, :], (SUBLANES, V7X_LANES)) for k in range(taps)]
        bias = bdw_ref[sl]

        def row_body(rb, c2):
            r0 = rb * CONV_ROWS
            accs = [None] * nacc
            for o in range(lead, lead + taps + CONV_ROWS - SUBLANES):
                x = slab_ref[p, pl.ds(2 * (r0 + o) + par, SUBLANES, stride=2), :]
                for i in range(nacc):
                    k = o - lead - SUBLANES * i
                    if 0 <= k < taps:
                        t = x * wk[k]
                        accs[i] = t if accs[i] is None else accs[i] + t
            for i in range(nacc):
                rows = pl.ds(pl.multiple_of(r0 + SUBLANES * i, SUBLANES), SUBLANES)
                cv_ref[sl, rows, :] = accs[i] + bias
            return c2

        lax.fori_loop(0, ts // CONV_ROWS, row_body, 0)
        return carry

    lax.fori_loop(0, ns, slab_body, 0)

    def norm_body(rg, carry):
        rows = pl.ds(pl.multiple_of(rg * NORM_ROWS, NORM_ROWS), NORM_ROWS)
        y = cv_ref[:, rows, :]
        mu = jnp.sum(jnp.sum(y, axis=0), axis=-1, keepdims=True) * (1.0 / C)
        yc = y - mu[None]
        var = jnp.sum(jnp.sum(yc * yc, axis=0), axis=-1, keepdims=True) * (1.0 / C)
        rs = lax.rsqrt(var + LN_EPS)
        for sl in range(ns):
            ls = slice(sl * V7X_LANES, (sl + 1) * V7X_LANES)
            yn = yc[sl] * rs * g_ref[:, ls] + b_ref[:, ls]
            o_ref[rows, ls] = (_silu(yn) * sz_ref[rows, ls]).astype(BF16)
        return carry

    lax.fori_loop(0, ts // NORM_ROWS, norm_body, 0)


def _conv_norm(v, sz, w_dw, b_dw, g_cn, b_cn, B, S):
    T, C = v.shape
    taps = w_dw.shape[0]
    ns = C // V7X_LANES
    assert taps - 1 <= CONV_HALO and ns % 2 == 0
    ts = _tile(S, 256)
    assert ts % CONV_HALO == 0 and ts % CONV_ROWS == 0 and ts % NORM_ROWS == 0
    per_b = S // ts
    hb = ts // CONV_HALO
    v3 = v.reshape(B, S, C)
    w_slab = w_dw.reshape(taps, ns, V7X_LANES).transpose(1, 0, 2)
    b_slab = b_dw.reshape(ns, 1, V7X_LANES)
    kern = functools.partial(_conv_kernel, taps=taps, ts=ts, C=C)
    row = lambda a: a.reshape(1, C)
    return pl.pallas_call(
        kern,
        grid=(B, per_b),
        in_specs=[
            pl.BlockSpec((1, CONV_HALO, C), lambda b, s: (b, jnp.maximum(s * hb - 1, 0), 0)),
            pl.BlockSpec((1, ts, C), lambda b, s: (b, s, 0)),
            pl.BlockSpec((ts, C), lambda b, s: (b * per_b + s, 0)),
            pl.BlockSpec((ns, taps, V7X_LANES), lambda b, s: (0, 0, 0)),
            pl.BlockSpec((ns, 1, V7X_LANES), lambda b, s: (0, 0, 0)),
            pl.BlockSpec((1, C), lambda b, s: (0, 0)),
            pl.BlockSpec((1, C), lambda b, s: (0, 0)),
        ],
        out_specs=pl.BlockSpec((ts, C), lambda b, s: (b * per_b + s, 0)),
        out_shape=jax.ShapeDtypeStruct((T, C), BF16),
        scratch_shapes=[pltpu.VMEM((ns // 2, 2 * (ts + CONV_HALO), V7X_LANES), F32),
                        pltpu.VMEM((ns, ts, V7X_LANES), F32)],
        compiler_params=_params("parallel", "parallel"),
        name="conv_norm",
    )(v3, v3, sz, w_slab, b_slab, row(g_cn), row(b_cn))


OUT_NORM_TK = 4096


def _out_kernel(act_ref, w_ref, x_ref, mod_ref, g_ref, b_ref, o_ref, y_ref,
                *, alpha, nk, nj, tn, D):
    kk = pl.program_id(1)
    j = pl.program_id(2)
    out = _dot(act_ref[...], w_ref[...])

    def residual(total):
        gate = 1.0 + mod_ref[0, 2:3, :]
        y_ref[j] = alpha * x_ref[...] + gate * total

    if nk == 1:
        residual(out)
    else:
        @pl.when(kk == 0)
        def _():
            y_ref[j] = out

        @pl.when(jnp.logical_and(kk > 0, kk < nk - 1))
        def _():
            y_ref[j] += out

        @pl.when(kk == nk - 1)
        def _():
            residual(y_ref[j] + out)

    @pl.when(jnp.logical_and(kk == nk - 1, j == nj - 1))
    def _():
        tot = jnp.sum(y_ref[0], axis=-1, keepdims=True)
        for t in range(1, nj):
            tot = tot + jnp.sum(y_ref[t], axis=-1, keepdims=True)
        mu = tot * (1.0 / D)
        sq = jnp.zeros_like(mu)
        for t in range(nj):
            d = y_ref[t] - mu
            sq = sq + jnp.sum(d * d, axis=-1, keepdims=True)
        rs = lax.rsqrt(sq * (1.0 / D) + LN_EPS)
        for t in range(nj):
            cs = slice(t * tn, (t + 1) * tn)
            o_ref[:, cs] = (y_ref[t] - mu) * rs * g_ref[:, cs] + b_ref[:, cs]


def _out_norm(act, w_out, x2, mod, ln_g, ln_b, alpha, S):
    T, K = act.shape
    D = w_out.shape[1]
    tm = _tile(S, 512)
    tn = _tile(D, 512)
    tk = _tile(K, OUT_NORM_TK)
    nj = D // tn
    nk = K // tk
    per_b = S // tm
    kern = functools.partial(_out_kernel, alpha=alpha, nk=nk, nj=nj, tn=tn, D=D)
    return pl.pallas_call(
        kern,
        grid=(T // tm, nk, nj),
        in_specs=[
            pl.BlockSpec((tm, tk), lambda i, k, j: (i, k)),
            pl.BlockSpec((tk, tn), lambda i, k, j: (k, j)),
            pl.BlockSpec((tm, tn), lambda i, k, j: (i, j)),
            pl.BlockSpec((1, 3, tn), lambda i, k, j: (i // per_b, 0, j)),
            pl.BlockSpec((1, D), lambda i, k, j: (0, 0)),
            pl.BlockSpec((1, D), lambda i, k, j: (0, 0)),
        ],
        out_specs=pl.BlockSpec((tm, D), lambda i, k, j: (i, 0)),
        out_shape=jax.ShapeDtypeStruct((T, D), F32),
        scratch_shapes=[pltpu.VMEM((nj, tm, tn), F32)],
        compiler_params=_params("parallel", "arbitrary", "arbitrary"),
        name="out_norm",
    )(act, w_out, x2, mod, ln_g.reshape(1, D), ln_b.reshape(1, D))


def _kv_kernel(x_ref, wa_ref, g_ref, cs_ref, wb_ref, k_ref, v_ref, ckv_ref, kr_ref, *, R, hb):
    j = pl.program_id(1)

    @pl.when(j == 0)
    def _():
        kva = _dot(x_ref[...].astype(BF16), wa_ref[...])
        ckv = kva[:, :R]
        ms = jnp.mean(ckv * ckv, axis=-1, keepdims=True)
        ckv_ref[...] = (ckv * lax.rsqrt(ms + RMS_EPS) * g_ref[...]).astype(BF16)
        u = kva[:, R:] * cs_ref[...]
        kr_ref[...] = (u[:, :QK_ROPE_DIM] + u[:, QK_ROPE_DIM:]).astype(BF16)

    kv = _dot(ckv_ref[...], wb_ref[...])
    per = QK_NOPE_DIM + V_HEAD_DIM
    for h in range(hb):
        k_ref[0, h, :, 0:QK_NOPE_DIM] = kv[:, h * per:h * per + QK_NOPE_DIM].astype(BF16)
        k_ref[0, h, :, QK_NOPE_DIM:] = kr_ref[...]
        v_ref[0, h] = kv[:, h * per + QK_NOPE_DIM:(h + 1) * per].astype(BF16)


def _kv_proj(x2, w_kva_ext, g_kv, cs_tab, w_kvb, B, S, H):
    T, D = x2.shape
    R = g_kv.shape[0]
    tm = _tile(S, 512)
    hb = _tile(H, 8)
    per_b = S // tm
    per = QK_NOPE_DIM + V_HEAD_DIM
    qk = QK_NOPE_DIM + QK_ROPE_DIM
    kern = functools.partial(_kv_kernel, R=R, hb=hb)
    return pl.pallas_call(
        kern,
        grid=(T // tm, H // hb),
        in_specs=[
            pl.BlockSpec((tm, D), lambda i, j: (i, 0)),
            pl.BlockSpec((D, R + 2 * QK_ROPE_DIM), lambda i, j: (0, 0)),
            pl.BlockSpec((1, R), lambda i, j: (0, 0)),
            pl.BlockSpec((tm, 2 * QK_ROPE_DIM), lambda i, j: (i % per_b, 0)),
            pl.BlockSpec((R, hb * per), lambda i, j: (0, j)),
        ],
        out_specs=[
            pl.BlockSpec((1, hb, tm, qk), lambda i, j: (i // per_b, j, i % per_b, 0)),
            pl.BlockSpec((1, hb, tm, V_HEAD_DIM), lambda i, j: (i // per_b, j, i % per_b, 0)),
        ],
        out_shape=[jax.ShapeDtypeStruct((B, H, S, qk), BF16),
                   jax.ShapeDtypeStruct((B, H, S, V_HEAD_DIM), BF16)],
        scratch_shapes=[pltpu.VMEM((tm, R), BF16), pltpu.VMEM((tm, QK_ROPE_DIM), BF16)],
        compiler_params=_params("parallel", "arbitrary"),
        name="kv_proj",
    )(x2, w_kva_ext, g_kv.reshape(1, R), cs_tab, w_kvb)


def _q_kernel(h_ref, wc_ref, g_ref, cos_ref, sin_ref, wn_ref, wr_ref, q_ref, cq_ref, *, hb, scale):
    j = pl.program_id(1)

    @pl.when(j == 0)
    def _():
        cq = _dot(h_ref[...], wc_ref[...])
        ms = jnp.mean(cq * cq, axis=-1, keepdims=True)
        cq_ref[...] = (cq * lax.rsqrt(ms + RMS_EPS) * g_ref[...]).astype(BF16)

    cq = cq_ref[...]
    nope = _dot(cq, wn_ref[...]) * scale
    t = _dot(cq, wr_ref[...]) * scale
    W = t.shape[1]
    half = QK_ROPE_DIM // 2
    lane = lax.broadcasted_iota(jnp.int32, t.shape, 1)
    first = (lane % QK_ROPE_DIM) < half
    partner = jnp.where(first, pltpu.roll(t, W - half, 1), pltpu.roll(t, half, 1))
    reps = W // V7X_LANES
    cos = jnp.concatenate([cos_ref[...]] * reps, axis=1) if reps > 1 else cos_ref[...]
    sin = jnp.concatenate([sin_ref[...]] * reps, axis=1) if reps > 1 else sin_ref[...]
    rot = t * cos + partner * sin
    for h in range(hb):
        q_ref[0, h, :, 0:QK_NOPE_DIM] = nope[:, h * QK_NOPE_DIM:(h + 1) * QK_NOPE_DIM].astype(BF16)
        q_ref[0, h, :, QK_NOPE_DIM:] = rot[:, h * QK_ROPE_DIM:(h + 1) * QK_ROPE_DIM].astype(BF16)


def _q_proj(h, w_cq, g_q, cos_tab, sin_tab, w_nope, w_rope, B, S, H, scale):
    T, D = h.shape
    R = g_q.shape[0]
    tm = _tile(S, 512)
    hb = _tile(H, 8)
    per_b = S // tm
    qk = QK_NOPE_DIM + QK_ROPE_DIM
    kern = functools.partial(_q_kernel, hb=hb, scale=scale)
    return pl.pallas_call(
        kern,
        grid=(T // tm, H // hb),
        in_specs=[
            pl.BlockSpec((tm, D), lambda i, j: (i, 0)),
            pl.BlockSpec((D, R), lambda i, j: (0, 0)),
            pl.BlockSpec((1, R), lambda i, j: (0, 0)),
            pl.BlockSpec((tm, V7X_LANES), lambda i, j: (i % per_b, 0)),
            pl.BlockSpec((tm, V7X_LANES), lambda i, j: (i % per_b, 0)),
            pl.BlockSpec((R, hb * QK_NOPE_DIM), lambda i, j: (0, j)),
            pl.BlockSpec((R, hb * QK_ROPE_DIM), lambda i, j: (0, j)),
        ],
        out_specs=pl.BlockSpec((1, hb, tm, qk), lambda i, j: (i // per_b, j, i % per_b, 0)),
        out_shape=jax.ShapeDtypeStruct((B, H, S, qk), BF16),
        scratch_shapes=[pltpu.VMEM((tm, R), BF16)],
        compiler_params=_params("parallel", "arbitrary"),
        name="q_proj",
    )(h, w_cq, g_q.reshape(1, R), cos_tab, sin_tab, w_nope, w_rope)


def _silu_proj_kernel(h_ref, w_ref, o_ref):
    o_ref[...] = _silu(_dot(h_ref[...], w_ref[...]))


def _silu_proj(h, w_z):
    T, D = h.shape
    N = w_z.shape[1]
    tm = _tile(T, 1024)
    tn = _tile(N, 512)
    return pl.pallas_call(
        _silu_proj_kernel,
        grid=(T // tm, N // tn),
        in_specs=[
            pl.BlockSpec((tm, D), lambda i, j: (i, 0)),
            pl.BlockSpec((D, tn), lambda i, j: (0, j)),
        ],
        out_specs=pl.BlockSpec((tm, tn), lambda i, j: (i, j)),
        out_shape=jax.ShapeDtypeStruct((T, N), F32),
        compiler_params=_params("parallel", "arbitrary"),
        name="silu_proj",
    )(h, w_z)


def _attn_kernel(q_ref, k_ref, v_ref, sz_ref, o_ref, m_ref, l_ref, acc_ref, *, tq, tk):
    qi = pl.program_id(2)
    q = q_ref[0, 0]
    m_ref[...] = jnp.full(m_ref.shape, -jnp.inf, F32)
    l_ref[...] = jnp.zeros(l_ref.shape, F32)
    acc_ref[...] = jnp.zeros(acc_ref.shape, F32)

    def step(k0, masked):
        k = k_ref[0, 0, pl.ds(k0, tk), :]
        s = lax.dot_general(q, k, (((1,), (1,)), ((), ())), preferred_element_type=F32)
        if masked:
            row = qi * tq + lax.broadcasted_iota(jnp.int32, s.shape, 0)
            col = k0 + lax.broadcasted_iota(jnp.int32, s.shape, 1)
            s = jnp.where(row >= col, s, -jnp.inf)
        m_prev = m_ref[...]
        m_new = jnp.maximum(m_prev, jnp.max(s, axis=-1, keepdims=True))
        alpha = jnp.exp(m_prev - m_new)
        p = jnp.exp(s - m_new)
        l_ref[...] = alpha * l_ref[...] + jnp.sum(p, axis=-1, keepdims=True)
        pv = _dot(p.astype(BF16), v_ref[0, 0, pl.ds(k0, tk), :])
        acc_ref[...] = alpha * acc_ref[...] + pv
        m_ref[...] = m_new

    def body(ki, carry):
        step(pl.multiple_of(ki * tk, tk), False)
        return carry

    nfull = qi * (tq // tk)
    lax.fori_loop(0, nfull, body, 0)
    for d in range(tq // tk):
        step(pl.multiple_of(qi * tq + d * tk, tk), True)

    o = acc_ref[...] / l_ref[...]
    o_ref[...] = (o * sz_ref[...]).astype(BF16)


def _attention(q, k, v, sz, B, S, H):
    tq = _tile(S, 512)
    tk = _tile(S, 512)
    nq = S // tq
    qk = q.shape[-1]
    kern = functools.partial(_attn_kernel, tq=tq, tk=tk)
    return pl.pallas_call(
        kern,
        grid=(B, H, nq),
        in_specs=[
            pl.BlockSpec((1, 1, tq, qk), lambda b, h, i: (b, h, i, 0)),
            pl.BlockSpec((1, 1, S, qk), lambda b, h, i: (b, h, 0, 0)),
            pl.BlockSpec((1, 1, S, V_HEAD_DIM), lambda b, h, i: (b, h, 0, 0)),
            pl.BlockSpec((tq, V_HEAD_DIM), lambda b, h, i: (b * nq + i, h)),
        ],
        out_specs=pl.BlockSpec((tq, V_HEAD_DIM), lambda b, h, i: (b * nq + i, h)),
        out_shape=jax.ShapeDtypeStruct((B * S, H * V_HEAD_DIM), BF16),
        scratch_shapes=[pltpu.VMEM((tq, 1), F32), pltpu.VMEM((tq, 1), F32),
                        pltpu.VMEM((tq, V_HEAD_DIM), F32)],
        compiler_params=_params("parallel", "parallel", "arbitrary"),
        name="attention",
    )(q, k, v, sz)


def _rope_tables(S):
    half = QK_ROPE_DIM // 2
    inv_freq = ROPE_BASE ** (-jnp.arange(half, dtype=F32) / half)
    ang = jnp.arange(S, dtype=jnp.int32).astype(F32)[:, None] * inv_freq[None, :]
    return jnp.cos(ang), jnp.sin(ang)


def kernel(x, c, w_ada, b_ada, ln_g, ln_b, a_w_in, a_w_dw, a_b_dw, a_norm_g, a_norm_b, a_w_out,
           b_w_in, b_q_norm_g, b_w_qb, b_w_out, kv_w_a, kv_norm_g, kv_w_b):
    B, S, D = x.shape
    depth = w_ada.shape[0]
    n_a = a_w_in.shape[0]
    alpha = (2.0 * depth) ** 0.25
    H = b_w_out.shape[1] // V_HEAD_DIM
    q_rank = b_q_norm_g.shape[1]
    kv_rank = kv_norm_g.shape[0]
    qk = QK_NOPE_DIM + QK_ROPE_DIM
    half = QK_ROPE_DIM // 2

    mod = _ada_mod(c, w_ada, b_ada)
    cos, sin = _rope_tables(S)
    k_tab = jnp.concatenate([cos, cos, -sin, sin], axis=1)
    q_cos = jnp.concatenate([cos, cos, cos, cos], axis=1)
    q_sin = jnp.concatenate([-sin, sin, -sin, sin], axis=1)

    x2 = x.reshape(B * S, D)
    kv = None
    for layer in range(depth):
        if layer == n_a:
            rope_w = kv_w_a[:, kv_rank:]
            w_kva_ext = jnp.concatenate(
                [kv_w_a, rope_w[:, half:], rope_w[:, :half]], axis=1).astype(BF16)
            kv = _kv_proj(x2, w_kva_ext, kv_norm_g, k_tab, kv_w_b.astype(BF16), B, S, H)
        h = _ln_modulate(x2, mod[layer], S)
        if layer < n_a:
            i = layer
            v, sz = _glu_proj(h, a_w_in[i].astype(BF16))
            act = _conv_norm(v, sz, a_w_dw[i], a_b_dw[i], a_norm_g[i], a_norm_b[i], B, S)
            w_out = a_w_out[i]
        else:
            j = layer - n_a
            w_in = b_w_in[j]
            w_qb = b_w_qb[j].reshape(q_rank, H, qk)
            w_nope = w_qb[:, :, :QK_NOPE_DIM].reshape(q_rank, H * QK_NOPE_DIM).astype(BF16)
            w_rope = w_qb[:, :, QK_NOPE_DIM:].reshape(q_rank, H * QK_ROPE_DIM).astype(BF16)
            q = _q_proj(h, w_in[:, :q_rank].astype(BF16), b_q_norm_g[j], q_cos, q_sin,
                        w_nope, w_rope, B, S, H, qk ** -0.5)
            sz = _silu_proj(h, w_in[:, q_rank:].astype(BF16))
            act = _attention(q, kv[0], kv[1], sz, B, S, H)
            w_out = b_w_out[j]
        x2 = _out_norm(act, w_out.astype(BF16), x2, mod[layer], ln_g[layer], ln_b[layer], alpha, S)
    return x2.reshape(B, S, D)
```

```python
import functools
import math

import jax
import jax.numpy as jnp
from jax import lax
from jax.experimental import pallas as pl
from jax.experimental.pallas import tpu as pltpu

QK_NOPE_DIM = 128
QK_ROPE_DIM = 64
V_HEAD_DIM = 128
ROPE_BASE = 10000.0
LN_EPS = 1e-5
RMS_EPS = 1e-6

V7X_LANES = 128
V7X_VMEM_BYTES = 64 * 1024 * 1024
VMEM_LIMIT_BYTES = 56 * 1024 * 1024

F32 = jnp.float32
BF16 = jnp.bfloat16


def _params(*semantics):
    return pltpu.CompilerParams(dimension_semantics=semantics,
                                vmem_limit_bytes=VMEM_LIMIT_BYTES)


def _tile(dim, want):
    t = min(dim, want)
    while dim % t:
        t //= 2
    return t


def _dot(a, b):
    return jnp.dot(a, b, preferred_element_type=F32)


def _silu(x):
    return x * jax.nn.sigmoid(x)


def _ada_kernel(c_ref, w_ref, b_ref, o_ref):
    sc = _silu(c_ref[...]).astype(BF16)
    o_ref[0] = _dot(sc, w_ref[0].astype(BF16)) + b_ref[0]


def _ada_mod(c, w_ada, b_ada):
    L, D, N = w_ada.shape
    B = c.shape[0]
    Bp = max(8, B)
    cp = jnp.pad(c, ((0, Bp - B), (0, 0)))
    tn = _tile(N, 512)
    out = pl.pallas_call(
        _ada_kernel,
        grid=(L, N // tn),
        in_specs=[
            pl.BlockSpec((Bp, D), lambda l, j: (0, 0)),
            pl.BlockSpec((1, D, tn), lambda l, j: (l, 0, j)),
            pl.BlockSpec((1, 1, tn), lambda l, j: (l, 0, j)),
        ],
        out_specs=pl.BlockSpec((1, Bp, tn), lambda l, j: (l, 0, j)),
        out_shape=jax.ShapeDtypeStruct((L, Bp, N), F32),
        compiler_params=_params("parallel", "parallel"),
        name="ada_mod",
    )(cp, w_ada, b_ada.reshape(L, 1, N))
    return out[:, :B].reshape(L, B, 3, D)


def _ln_mod_kernel(x_ref, mod_ref, h_ref):
    x = x_ref[...]
    mu = jnp.mean(x, axis=-1, keepdims=True)
    xc = x - mu
    var = jnp.mean(xc * xc, axis=-1, keepdims=True)
    y = xc * lax.rsqrt(var + LN_EPS)
    shift = mod_ref[0, 0:1, :]
    scale = mod_ref[0, 1:2, :]
    h_ref[...] = (y * (1.0 + scale) + shift).astype(BF16)


def _ln_modulate(x2, mod, S):
    T, D = x2.shape
    tm = _tile(S, 256)
    per_b = S // tm
    return pl.pallas_call(
        _ln_mod_kernel,
        grid=(T // tm,),
        in_specs=[
            pl.BlockSpec((tm, D), lambda i: (i, 0)),
            pl.BlockSpec((1, 3, D), lambda i: (i // per_b, 0, 0)),
        ],
        out_specs=pl.BlockSpec((tm, D), lambda i: (i, 0)),
        out_shape=jax.ShapeDtypeStruct((T, D), BF16),
        compiler_params=_params("parallel"),
        name="ln_modulate",
    )(x2, mod)


def _glu_kernel(h_ref, wa_ref, wg_ref, wz_ref, v_ref, sz_ref):
    h = h_ref[...]
    a = _dot(h, wa_ref[...])
    g = _dot(h, wg_ref[...])
    v_ref[...] = a * jax.nn.sigmoid(g)
    z = _dot(h, wz_ref[...])
    sz_ref[...] = _silu(z)


def _glu_proj(h, w_in):
    T, D = h.shape
    C = w_in.shape[1] // 3
    tm = _tile(T, 1024)
    tn = _tile(C, 256)
    nj = C // tn
    return pl.pallas_call(
        _glu_kernel,
        grid=(T // tm, nj),
        in_specs=[
            pl.BlockSpec((tm, D), lambda i, j: (i, 0)),
            pl.BlockSpec((D, tn), lambda i, j: (0, j)),
            pl.BlockSpec((D, tn), lambda i, j: (0, j + nj)),
            pl.BlockSpec((D, tn), lambda i, j: (0, j + 2 * nj)),
        ],
        out_specs=[
            pl.BlockSpec((tm, tn), lambda i, j: (i, j)),
            pl.BlockSpec((tm, tn), lambda i, j: (i, j)),
        ],
        out_shape=[jax.ShapeDtypeStruct((T, C), F32), jax.ShapeDtypeStruct((T, C), F32)],
        compiler_params=_params("parallel", "arbitrary"),
        name="glu_proj",
    )(h, w_in, w_in, w_in)


CONV_HALO = 32
CONV_ROWS = 64
NORM_ROWS = 16
SUBLANES = 8


def _conv_kernel(vh_ref, vm_ref, sz_ref, w_ref, bdw_ref, g_ref, b_ref, o_ref, slab_ref, cv_ref,
                 *, taps, ts, C):
    s = pl.program_id(1)
    ns = C // V7X_LANES
    lead = CONV_HALO - (taps - 1)
    for sl in range(ns):
        p, par = divmod(sl, 2)
        ls = slice(sl * V7X_LANES, (sl + 1) * V7X_LANES)
        halo = vh_ref[0, :, ls]
        slab_ref[p, pl.ds(par, CONV_HALO, stride=2), :] = jnp.where(s > 0, halo, jnp.zeros_like(halo))
        slab_ref[p, pl.ds(2 * CONV_HALO + par, ts, stride=2), :] = vm_ref[0, :, ls]

    nacc = CONV_ROWS // SUBLANES

    def slab_body(sl, carry):
        p = sl // 2
        par = sl % 2
        wk = [jnp.broadcast_to(w_ref[sl, k:k + 1, :], (SUBLANES, V7X_LANES)) for k in range(taps)]
        bias = bdw_ref[sl]

        def row_body(rb, c2):
            r0 = rb * CONV_ROWS
            accs = [None] * nacc
            for o in range(lead, lead + taps + CONV_ROWS - SUBLANES):
                x = slab_ref[p, pl.ds(2 * (r0 + o) + par, SUBLANES, stride=2), :]
                for i in range(nacc):
                    k = o - lead - SUBLANES * i
                    if 0 <= k < taps:
                        t = x * wk[k]
                        accs[i] = t if accs[i] is None else accs[i] + t
            for i in range(nacc):
                rows = pl.ds(pl.multiple_of(r0 + SUBLANES * i, SUBLANES), SUBLANES)
                cv_ref[sl, rows, :] = accs[i] + bias
            return c2

        lax.fori_loop(0, ts // CONV_ROWS, row_body, 0)
        return carry

    lax.fori_loop(0, ns, slab_body, 0)

    def norm_body(rg, carry):
        rows = pl.ds(pl.multiple_of(rg * NORM_ROWS, NORM_ROWS), NORM_ROWS)
        y = cv_ref[:, rows, :]
        mu = jnp.sum(jnp.sum(y, axis=0), axis=-1, keepdims=True) * (1.0 / C)
        yc = y - mu[None]
        var = jnp.sum(jnp.sum(yc * yc, axis=0), axis=-1, keepdims=True) * (1.0 / C)
        rs = lax.rsqrt(var + LN_EPS)
        for sl in range(ns):
            ls = slice(sl * V7X_LANES, (sl + 1) * V7X_LANES)
            yn = yc[sl] * rs * g_ref[:, ls] + b_ref[:, ls]
            o_ref[rows, ls] = (_silu(yn) * sz_ref[rows, ls]).astype(BF16)
        return carry

    lax.fori_loop(0, ts // NORM_ROWS, norm_body, 0)


def _conv_norm(v, sz, w_dw, b_dw, g_cn, b_cn, B, S):
    T, C = v.shape
    taps = w_dw.shape[0]
    ns = C // V7X_LANES
    assert taps - 1 <= CONV_HALO and ns % 2 == 0
    ts = _tile(S, 256)
    assert ts % CONV_HALO == 0 and ts % CONV_ROWS == 0 and ts % NORM_ROWS == 0
    per_b = S // ts
    hb = ts // CONV_HALO
    v3 = v.reshape(B, S, C)
    w_slab = w_dw.reshape(taps, ns, V7X_LANES).transpose(1, 0, 2)
    b_slab = b_dw.reshape(ns, 1, V7X_LANES)
    kern = functools.partial(_conv_kernel, taps=taps, ts=ts, C=C)
    row = lambda a: a.reshape(1, C)
    return pl.pallas_call(
        kern,
        grid=(B, per_b),
        in_specs=[
            pl.BlockSpec((1, CONV_HALO, C), lambda b, s: (b, jnp.maximum(s * hb - 1, 0), 0)),
            pl.BlockSpec((1, ts, C), lambda b, s: (b, s, 0)),
            pl.BlockSpec((ts, C), lambda b, s: (b * per_b + s, 0)),
            pl.BlockSpec((ns, taps, V7X_LANES), lambda b, s: (0, 0, 0)),
            pl.BlockSpec((ns, 1, V7X_LANES), lambda b, s: (0, 0, 0)),
            pl.BlockSpec((1, C), lambda b, s: (0, 0)),
            pl.BlockSpec((1, C), lambda b, s: (0, 0)),
        ],
        out_specs=pl.BlockSpec((ts, C), lambda b, s: (b * per_b + s, 0)),
        out_shape=jax.ShapeDtypeStruct((T, C), BF16),
        scratch_shapes=[pltpu.VMEM((ns // 2, 2 * (ts + CONV_HALO), V7X_LANES), F32),
                        pltpu.VMEM((ns, ts, V7X_LANES), F32)],
        compiler_params=_params("parallel", "parallel"),
        name="conv_norm",
    )(v3, v3, sz, w_slab, b_slab, row(g_cn), row(b_cn))


OUT_NORM_TK = 4096


def _out_kernel(act_ref, w_ref, x_ref, mod_ref, g_ref, b_ref, o_ref, y_ref,
                *, alpha, nk, nj, tn, D):
    kk = pl.program_id(1)
    j = pl.program_id(2)
    out = _dot(act_ref[...], w_ref[...])

    def residual(total):
        gate = 1.0 + mod_ref[0, 2:3, :]
        y_ref[j] = alpha * x_ref[...] + gate * total

    if nk == 1:
        residual(out)
    else:
        @pl.when(kk == 0)
        def _():
            y_ref[j] = out

        @pl.when(jnp.logical_and(kk > 0, kk < nk - 1))
        def _():
            y_ref[j] += out

        @pl.when(kk == nk - 1)
        def _():
            residual(y_ref[j] + out)

    @pl.when(jnp.logical_and(kk == nk - 1, j == nj - 1))
    def _():
        tot = jnp.sum(y_ref[0], axis=-1, keepdims=True)
        for t in range(1, nj):
            tot = tot + jnp.sum(y_ref[t], axis=-1, keepdims=True)
        mu = tot * (1.0 / D)
        sq = jnp.zeros_like(mu)
        for t in range(nj):
            d = y_ref[t] - mu
            sq = sq + jnp.sum(d * d, axis=-1, keepdims=True)
        rs = lax.rsqrt(sq * (1.0 / D) + LN_EPS)
        for t in range(nj):
            cs = slice(t * tn, (t + 1) * tn)
            o_ref[:, cs] = (y_ref[t] - mu) * rs * g_ref[:, cs] + b_ref[:, cs]


def _out_norm(act, w_out, x2, mod, ln_g, ln_b, alpha, S):
    T, K = act.shape
    D = w_out.shape[1]
    tm = _tile(S, 512)
    tn = _tile(D, 512)
    tk = _tile(K, OUT_NORM_TK)
    nj = D // tn
    nk = K // tk
    per_b = S // tm
    kern = functools.partial(_out_kernel, alpha=alpha, nk=nk, nj=nj, tn=tn, D=D)
    return pl.pallas_call(
        kern,
        grid=(T // tm, nk, nj),
        in_specs=[
            pl.BlockSpec((tm, tk), lambda i, k, j: (i, k)),
            pl.BlockSpec((tk, tn), lambda i, k, j: (k, j)),
            pl.BlockSpec((tm, tn), lambda i, k, j: (i, j)),
            pl.BlockSpec((1, 3, tn), lambda i, k, j: (i // per_b, 0, j)),
            pl.BlockSpec((1, D), lambda i, k, j: (0, 0)),
            pl.BlockSpec((1, D), lambda i, k, j: (0, 0)),
        ],
        out_specs=pl.BlockSpec((tm, D), lambda i, k, j: (i, 0)),
        out_shape=jax.ShapeDtypeStruct((T, D), F32),
        scratch_shapes=[pltpu.VMEM((nj, tm, tn), F32)],
        compiler_params=_params("parallel", "arbitrary", "arbitrary"),
        name="out_norm",
    )(act, w_out, x2, mod, ln_g.reshape(1, D), ln_b.reshape(1, D))


def _kv_kernel(x_ref, wa_ref, g_ref, cs_ref, wk_ref, wvt_ref, k_ref, vt_ref, ckv_ref, kr_ref,
               *, R, hb):
    j = pl.program_id(1)

    @pl.when(j == 0)
    def _():
        kva = _dot(x_ref[...].astype(BF16), wa_ref[...])
        ckv = kva[:, :R]
        ms = jnp.mean(ckv * ckv, axis=-1, keepdims=True)
        ckv_ref[...] = (ckv * lax.rsqrt(ms + RMS_EPS) * g_ref[...]).astype(BF16)
        u = kva[:, R:] * cs_ref[...]
        kr_ref[...] = (u[:, :QK_ROPE_DIM] + u[:, QK_ROPE_DIM:]).astype(BF16)

    ckv = ckv_ref[...]
    kn = _dot(ckv, wk_ref[...])
    vt = lax.dot_general(wvt_ref[...], ckv, (((1,), (1,)), ((), ())), preferred_element_type=F32)
    for h in range(hb):
        k_ref[0, h, :, 0:QK_NOPE_DIM] = kn[:, h * QK_NOPE_DIM:(h + 1) * QK_NOPE_DIM].astype(BF16)
        k_ref[0, h, :, QK_NOPE_DIM:] = kr_ref[...]
        vt_ref[0, h] = vt[h * V_HEAD_DIM:(h + 1) * V_HEAD_DIM, :].astype(BF16)


def _kv_proj(x2, w_kva_ext, g_kv, cs_tab, w_k, w_vt, B, S, H):
    T, D = x2.shape
    R = g_kv.shape[0]
    tm = _tile(S, 512)
    hb = _tile(H, 8)
    per_b = S // tm
    qk = QK_NOPE_DIM + QK_ROPE_DIM
    kern = functools.partial(_kv_kernel, R=R, hb=hb)
    return pl.pallas_call(
        kern,
        grid=(T // tm, H // hb),
        in_specs=[
            pl.BlockSpec((tm, D), lambda i, j: (i, 0)),
            pl.BlockSpec((D, R + 2 * QK_ROPE_DIM), lambda i, j: (0, 0)),
            pl.BlockSpec((1, R), lambda i, j: (0, 0)),
            pl.BlockSpec((tm, 2 * QK_ROPE_DIM), lambda i, j: (i % per_b, 0)),
            pl.BlockSpec((R, hb * QK_NOPE_DIM), lambda i, j: (0, j)),
            pl.BlockSpec((hb * V_HEAD_DIM, R), lambda i, j: (j, 0)),
        ],
        out_specs=[
            pl.BlockSpec((1, hb, tm, qk), lambda i, j: (i // per_b, j, i % per_b, 0)),
            pl.BlockSpec((1, hb, V_HEAD_DIM, tm), lambda i, j: (i // per_b, j, 0, i % per_b)),
        ],
        out_shape=[jax.ShapeDtypeStruct((B, H, S, qk), BF16),
                   jax.ShapeDtypeStruct((B, H, V_HEAD_DIM, S), BF16)],
        scratch_shapes=[pltpu.VMEM((tm, R), BF16), pltpu.VMEM((tm, QK_ROPE_DIM), BF16)],
        compiler_params=_params("parallel", "arbitrary"),
        name="kv_proj",
    )(x2, w_kva_ext, g_kv.reshape(1, R), cs_tab, w_k, w_vt)


def _q_kernel(h_ref, wc_ref, g_ref, cos_ref, sin_ref, wn_ref, wr_ref, q_ref, cq_ref, *, hb, scale):
    j = pl.program_id(1)

    @pl.when(j == 0)
    def _():
        cq = _dot(h_ref[...], wc_ref[...])
        ms = jnp.mean(cq * cq, axis=-1, keepdims=True)
        cq_ref[...] = (cq * lax.rsqrt(ms + RMS_EPS) * g_ref[...]).astype(BF16)

    cq = cq_ref[...]
    nope = _dot(cq, wn_ref[...]) * scale
    t = _dot(cq, wr_ref[...]) * scale
    W = t.shape[1]
    half = QK_ROPE_DIM // 2
    lane = lax.broadcasted_iota(jnp.int32, t.shape, 1)
    first = (lane % QK_ROPE_DIM) < half
    partner = jnp.where(first, pltpu.roll(t, W - half, 1), pltpu.roll(t, half, 1))
    reps = W // V7X_LANES
    cos = jnp.concatenate([cos_ref[...]] * reps, axis=1) if reps > 1 else cos_ref[...]
    sin = jnp.concatenate([sin_ref[...]] * reps, axis=1) if reps > 1 else sin_ref[...]
    rot = t * cos + partner * sin
    for h in range(hb):
        q_ref[0, h, :, 0:QK_NOPE_DIM] = nope[:, h * QK_NOPE_DIM:(h + 1) * QK_NOPE_DIM].astype(BF16)
        q_ref[0, h, :, QK_NOPE_DIM:] = rot[:, h * QK_ROPE_DIM:(h + 1) * QK_ROPE_DIM].astype(BF16)


def _q_proj(h, w_cq, g_q, cos_tab, sin_tab, w_nope, w_rope, B, S, H, scale):
    T, D = h.shape
    R = g_q.shape[0]
    tm = _tile(S, 512)
    hb = _tile(H, 8)
    per_b = S // tm
    qk = QK_NOPE_DIM + QK_ROPE_DIM
    kern = functools.partial(_q_kernel, hb=hb, scale=scale)
    return pl.pallas_call(
        kern,
        grid=(T // tm, H // hb),
        in_specs=[
            pl.BlockSpec((tm, D), lambda i, j: (i, 0)),
            pl.BlockSpec((D, R), lambda i, j: (0, 0)),
            pl.BlockSpec((1, R), lambda i, j: (0, 0)),
            pl.BlockSpec((tm, V7X_LANES), lambda i, j: (i % per_b, 0)),
            pl.BlockSpec((tm, V7X_LANES), lambda i, j: (i % per_b, 0)),
            pl.BlockSpec((R, hb * QK_NOPE_DIM), lambda i, j: (0, j)),
            pl.BlockSpec((R, hb * QK_ROPE_DIM), lambda i, j: (0, j)),
        ],
        out_specs=pl.BlockSpec((1, hb, tm, qk), lambda i, j: (i // per_b, j, i % per_b, 0)),
        out_shape=jax.ShapeDtypeStruct((B, H, S, qk), BF16),
        scratch_shapes=[pltpu.VMEM((tm, R), BF16)],
        compiler_params=_params("parallel", "arbitrary"),
        name="q_proj",
    )(h, w_cq, g_q.reshape(1, R), cos_tab, sin_tab, w_nope, w_rope)


def _silu_proj_kernel(h_ref, w_ref, o_ref):
    o_ref[...] = _silu(_dot(h_ref[...], w_ref[...]))


def _silu_proj(h, w_z):
    T, D = h.shape
    N = w_z.shape[1]
    tm = _tile(T, 1024)
    tn = _tile(N, 512)
    return pl.pallas_call(
        _silu_proj_kernel,
        grid=(T // tm, N // tn),
        in_specs=[
            pl.BlockSpec((tm, D), lambda i, j: (i, 0)),
            pl.BlockSpec((D, tn), lambda i, j: (0, j)),
        ],
        out_specs=pl.BlockSpec((tm, tn), lambda i, j: (i, j)),
        out_shape=jax.ShapeDtypeStruct((T, N), F32),
        compiler_params=_params("parallel", "arbitrary"),
        name="silu_proj",
    )(h, w_z)


ATTN_HEADS_PER_STEP = 4
ATTN_EXP_ROWS = 16


def _attn_kernel(q_ref, k_ref, vt_ref, sz_ref, bias_ref, o_ref,
                 m_ref, l_ref, acc_ref, s_ref, p_ref, mx_ref, al_ref, *, tq, tk, G):
    qi = pl.program_id(2)
    m_ref[...] = jnp.full(m_ref.shape, -jnp.inf, F32)
    l_ref[...] = jnp.zeros(l_ref.shape, F32)
    acc_ref[...] = jnp.zeros(acc_ref.shape, F32)
    n = G * (qi + 1)
    lg = G.bit_length() - 1
    ch = ATTN_EXP_ROWS

    def item(it):
        g = jnp.bitwise_and(it, G - 1)
        ki = jnp.right_shift(it, lg)
        return g, ki, pl.multiple_of(ki * tk, tk)

    def scores(it, slot):
        g, ki, k0 = item(it)
        k = k_ref[0, g, pl.ds(k0, tk), :]
        q = q_ref[0, g]
        s = lax.dot_general(k, q, (((1,), (1,)), ((), ())), preferred_element_type=F32)
        s = s + bias_ref[(ki == qi).astype(jnp.int32)]
        s_ref[slot] = s
        mx_ref[slot] = jnp.max(s.reshape(tk // SUBLANES, SUBLANES, tq), axis=0)

    def softmax(it, slot):
        g, _, _ = item(it)
        m_prev = m_ref[g]
        m_new = jnp.maximum(m_prev, jnp.max(mx_ref[slot], axis=0, keepdims=True))
        alpha = jnp.exp2(m_prev - m_new)
        mb = jnp.broadcast_to(m_new, (ch, tq))
        ls = None
        for c in range(tk // ch):
            p = jnp.exp2(s_ref[slot, c * ch:(c + 1) * ch, :] - mb)
            ls = p if ls is None else ls + p
            p_ref[slot, c * ch:(c + 1) * ch, :] = p.astype(BF16)
        l_ref[g] = alpha * l_ref[g] + jnp.sum(ls, axis=0, keepdims=True)
        m_ref[g] = m_new
        al_ref[slot] = alpha

    def accumulate(it, slot):
        g, _, k0 = item(it)
        pv = _dot(vt_ref[0, g, :, pl.ds(k0, tk)], p_ref[slot])
        acc_ref[g] = al_ref[slot] * acc_ref[g] + pv

    scores(0, 0)
    scores(1, 1)
    softmax(0, 0)

    def body(t, carry):
        slot = jnp.bitwise_and(t, 1)
        softmax(t - 1, 1 - slot)
        scores(t, slot)
        accumulate(t - 2, slot)
        return carry

    lax.fori_loop(2, n, body, 0)
    last = jnp.bitwise_and(n, 1)
    accumulate(n - 2, last)
    softmax(n - 1, 1 - last)
    accumulate(n - 1, 1 - last)

    for g in range(G):
        cs = slice(g * V_HEAD_DIM, (g + 1) * V_HEAD_DIM)
        o = (acc_ref[g] / l_ref[g]).T
        o_ref[:, cs] = (o * sz_ref[:, cs]).astype(BF16)


def _attention(q, k, vt, sz, B, S, H):
    tq = tk = _tile(S, 512)
    G = _tile(H, ATTN_HEADS_PER_STEP)
    assert G >= 2 and G & (G - 1) == 0
    nq = S // tq
    qk = q.shape[-1]
    r = lax.broadcasted_iota(jnp.int32, (tk, tq), 0)
    c = lax.broadcasted_iota(jnp.int32, (tk, tq), 1)
    bias = jnp.stack([jnp.zeros((tk, tq), F32), jnp.where(c >= r, 0.0, -jnp.inf).astype(F32)])
    kern = functools.partial(_attn_kernel, tq=tq, tk=tk, G=G)
    return pl.pallas_call(
        kern,
        grid=(B, H // G, nq),
        in_specs=[
            pl.BlockSpec((1, G, tq, qk), lambda b, h, i: (b, h, i, 0)),
            pl.BlockSpec((1, G, S, qk), lambda b, h, i: (b, h, 0, 0)),
            pl.BlockSpec((1, G, V_HEAD_DIM, S), lambda b, h, i: (b, h, 0, 0)),
            pl.BlockSpec((tq, G * V_HEAD_DIM), lambda b, h, i: (b * nq + i, h)),
            pl.BlockSpec((2, tk, tq), lambda b, h, i: (0, 0, 0)),
        ],
        out_specs=pl.BlockSpec((tq, G * V_HEAD_DIM), lambda b, h, i: (b * nq + i, h)),
        out_shape=jax.ShapeDtypeStruct((B * S, H * V_HEAD_DIM), BF16),
        scratch_shapes=[pltpu.VMEM((G, 1, tq), F32), pltpu.VMEM((G, 1, tq), F32),
                        pltpu.VMEM((G, V_HEAD_DIM, tq), F32),
                        pltpu.VMEM((2, tk, tq), F32), pltpu.VMEM((2, tk, tq), BF16),
                        pltpu.VMEM((2, SUBLANES, tq), F32), pltpu.VMEM((2, 1, tq), F32)],
        compiler_params=_params("parallel", "parallel", "arbitrary"),
        name="attention",
    )(q, k, vt, sz, bias)


def _rope_tables(S):
    half = QK_ROPE_DIM // 2
    inv_freq = ROPE_BASE ** (-jnp.arange(half, dtype=F32) / half)
    ang = jnp.arange(S, dtype=jnp.int32).astype(F32)[:, None] * inv_freq[None, :]
    return jnp.cos(ang), jnp.sin(ang)


def kernel(x, c, w_ada, b_ada, ln_g, ln_b, a_w_in, a_w_dw, a_b_dw, a_norm_g, a_norm_b, a_w_out,
           b_w_in, b_q_norm_g, b_w_qb, b_w_out, kv_w_a, kv_norm_g, kv_w_b):
    B, S, D = x.shape
    depth = w_ada.shape[0]
    n_a = a_w_in.shape[0]
    alpha = (2.0 * depth) ** 0.25
    H = b_w_out.shape[1] // V_HEAD_DIM
    q_rank = b_q_norm_g.shape[1]
    kv_rank = kv_norm_g.shape[0]
    qk = QK_NOPE_DIM + QK_ROPE_DIM
    half = QK_ROPE_DIM // 2

    mod = _ada_mod(c, w_ada, b_ada)
    cos, sin = _rope_tables(S)
    k_tab = jnp.concatenate([cos, cos, -sin, sin], axis=1)
    q_cos = jnp.concatenate([cos, cos, cos, cos], axis=1)
    q_sin = jnp.concatenate([-sin, sin, -sin, sin], axis=1)

    x2 = x.reshape(B * S, D)
    kv = None
    for layer in range(depth):
        if layer == n_a:
            rope_w = kv_w_a[:, kv_rank:]
            w_kva_ext = jnp.concatenate(
                [kv_w_a, rope_w[:, half:], rope_w[:, :half]], axis=1).astype(BF16)
            w_kvb = kv_w_b.reshape(kv_rank, H, QK_NOPE_DIM + V_HEAD_DIM)
            w_k = w_kvb[:, :, :QK_NOPE_DIM].reshape(kv_rank, H * QK_NOPE_DIM).astype(BF16)
            w_vt = w_kvb[:, :, QK_NOPE_DIM:].reshape(kv_rank, H * V_HEAD_DIM).T.astype(BF16)
            kv = _kv_proj(x2, w_kva_ext, kv_norm_g, k_tab, w_k, w_vt, B, S, H)
        h = _ln_modulate(x2, mod[layer], S)
        if layer < n_a:
            i = layer
            v, sz = _glu_proj(h, a_w_in[i].astype(BF16))
            act = _conv_norm(v, sz, a_w_dw[i], a_b_dw[i], a_norm_g[i], a_norm_b[i], B, S)
            w_out = a_w_out[i]
        else:
            j = layer - n_a
            w_in = b_w_in[j]
            w_qb = b_w_qb[j].reshape(q_rank, H, qk)
            w_nope = w_qb[:, :, :QK_NOPE_DIM].reshape(q_rank, H * QK_NOPE_DIM).astype(BF16)
            w_rope = w_qb[:, :, QK_NOPE_DIM:].reshape(q_rank, H * QK_ROPE_DIM).astype(BF16)
            q = _q_proj(h, w_in[:, :q_rank].astype(BF16), b_q_norm_g[j], q_cos, q_sin,
                        w_nope, w_rope, B, S, H, qk ** -0.5 * math.log2(math.e))
            sz = _silu_proj(h, w_in[:, q_rank:].astype(BF16))
            act = _attention(q, kv[0], kv[1], sz, B, S, H)
            w_out = b_w_out[j]
        x2 = _out_norm(act, w_out.astype(BF16), x2, mod[layer], ln_g[layer], ln_b[layer], alpha, S)
    return x2.reshape(B, S, D)
```

```python
import functools
import math

import jax
import jax.numpy as jnp
from jax import lax
from jax.experimental import pallas as pl
from jax.experimental.pallas import tpu as pltpu

QK_NOPE_DIM = 128
QK_ROPE_DIM = 64
V_HEAD_DIM = 128
ROPE_BASE = 10000.0
LN_EPS = 1e-5
RMS_EPS = 1e-6

V7X_LANES = 128
V7X_VMEM_BYTES = 64 * 1024 * 1024
VMEM_LIMIT_BYTES = 56 * 1024 * 1024

F32 = jnp.float32
BF16 = jnp.bfloat16


def _params(*semantics):
    return pltpu.CompilerParams(dimension_semantics=semantics,
                                vmem_limit_bytes=VMEM_LIMIT_BYTES)


def _tile(dim, want):
    t = min(dim, want)
    while dim % t:
        t //= 2
    return t


def _dot(a, b):
    return jnp.dot(a, b, preferred_element_type=F32)


def _silu(x):
    return x * jax.nn.sigmoid(x)


def _ada_kernel(c_ref, w_ref, b_ref, o_ref):
    sc = _silu(c_ref[...]).astype(BF16)
    o_ref[0] = _dot(sc, w_ref[0].astype(BF16)) + b_ref[0]


def _ada_mod(c, w_ada, b_ada):
    L, D, N = w_ada.shape
    B = c.shape[0]
    Bp = max(8, B)
    cp = jnp.pad(c, ((0, Bp - B), (0, 0)))
    tn = _tile(N, 512)
    out = pl.pallas_call(
        _ada_kernel,
        grid=(L, N // tn),
        in_specs=[
            pl.BlockSpec((Bp, D), lambda l, j: (0, 0)),
            pl.BlockSpec((1, D, tn), lambda l, j: (l, 0, j)),
            pl.BlockSpec((1, 1, tn), lambda l, j: (l, 0, j)),
        ],
        out_specs=pl.BlockSpec((1, Bp, tn), lambda l, j: (l, 0, j)),
        out_shape=jax.ShapeDtypeStruct((L, Bp, N), F32),
        compiler_params=_params("parallel", "parallel"),
        name="ada_mod",
    )(cp, w_ada, b_ada.reshape(L, 1, N))
    return out[:, :B].reshape(L, B, 3, D)


def _ln_mod_kernel(x_ref, mod_ref, h_ref):
    x = x_ref[...]
    mu = jnp.mean(x, axis=-1, keepdims=True)
    xc = x - mu
    var = jnp.mean(xc * xc, axis=-1, keepdims=True)
    y = xc * lax.rsqrt(var + LN_EPS)
    shift = mod_ref[0, 0:1, :]
    scale = mod_ref[0, 1:2, :]
    h_ref[...] = (y * (1.0 + scale) + shift).astype(BF16)


def _ln_modulate(x2, mod, S):
    T, D = x2.shape
    tm = _tile(S, 256)
    per_b = S // tm
    return pl.pallas_call(
        _ln_mod_kernel,
        grid=(T // tm,),
        in_specs=[
            pl.BlockSpec((tm, D), lambda i: (i, 0)),
            pl.BlockSpec((1, 3, D), lambda i: (i // per_b, 0, 0)),
        ],
        out_specs=pl.BlockSpec((tm, D), lambda i: (i, 0)),
        out_shape=jax.ShapeDtypeStruct((T, D), BF16),
        compiler_params=_params("parallel"),
        name="ln_modulate",
    )(x2, mod)


def _glu_kernel(h_ref, wa_ref, wg_ref, wz_ref, v_ref, sz_ref):
    h = h_ref[...]
    a = _dot(h, wa_ref[...])
    g = _dot(h, wg_ref[...])
    v_ref[...] = a * jax.nn.sigmoid(g)
    z = _dot(h, wz_ref[...])
    sz_ref[...] = _silu(z)


def _glu_proj(h, w_in):
    T, D = h.shape
    C = w_in.shape[1] // 3
    tm = _tile(T, 1024)
    tn = _tile(C, 256)
    nj = C // tn
    return pl.pallas_call(
        _glu_kernel,
        grid=(T // tm, nj),
        in_specs=[
            pl.BlockSpec((tm, D), lambda i, j: (i, 0)),
            pl.BlockSpec((D, tn), lambda i, j: (0, j)),
            pl.BlockSpec((D, tn), lambda i, j: (0, j + nj)),
            pl.BlockSpec((D, tn), lambda i, j: (0, j + 2 * nj)),
        ],
        out_specs=[
            pl.BlockSpec((tm, tn), lambda i, j: (i, j)),
            pl.BlockSpec((tm, tn), lambda i, j: (i, j)),
        ],
        out_shape=[jax.ShapeDtypeStruct((T, C), F32), jax.ShapeDtypeStruct((T, C), F32)],
        compiler_params=_params("parallel", "arbitrary"),
        name="glu_proj",
    )(h, w_in, w_in, w_in)


CONV_HALO = 32
CONV_ROWS = 64
NORM_ROWS = 16
SUBLANES = 8


def _conv_kernel(vh_ref, vm_ref, sz_ref, w_ref, bdw_ref, g_ref, b_ref, o_ref, slab_ref, cv_ref,
                 *, taps, ts, C):
    s = pl.program_id(1)
    ns = C // V7X_LANES
    lead = CONV_HALO - (taps - 1)
    for sl in range(ns):
        p, par = divmod(sl, 2)
        ls = slice(sl * V7X_LANES, (sl + 1) * V7X_LANES)
        halo = vh_ref[0, :, ls]
        slab_ref[p, pl.ds(par, CONV_HALO, stride=2), :] = jnp.where(s > 0, halo, jnp.zeros_like(halo))
        slab_ref[p, pl.ds(2 * CONV_HALO + par, ts, stride=2), :] = vm_ref[0, :, ls]

    nacc = CONV_ROWS // SUBLANES

    def slab_body(sl, carry):
        p = sl // 2
        par = sl % 2
        wk = [jnp.broadcast_to(w_ref[sl, k:k + 1, :], (SUBLANES, V7X_LANES)) for k in range(taps)]
        bias = bdw_ref[sl]

        def row_body(rb, c2):
            r0 = rb * CONV_ROWS
            accs = [None] * nacc
            for o in range(lead, lead + taps + CONV_ROWS - SUBLANES):
                x = slab_ref[p, pl.ds(2 * (r0 + o) + par, SUBLANES, stride=2), :]
                for i in range(nacc):
                    k = o - lead - SUBLANES * i
                    if 0 <= k < taps:
                        t = x * wk[k]
                        accs[i] = t if accs[i] is None else accs[i] + t
            for i in range(nacc):
                rows = pl.ds(pl.multiple_of(r0 + SUBLANES * i, SUBLANES), SUBLANES)
                cv_ref[sl, rows, :] = accs[i] + bias
            return c2

        lax.fori_loop(0, ts // CONV_ROWS, row_body, 0)
        return carry

    lax.fori_loop(0, ns, slab_body, 0)

    def norm_body(rg, carry):
        rows = pl.ds(pl.multiple_of(rg * NORM_ROWS, NORM_ROWS), NORM_ROWS)
        y = cv_ref[:, rows, :]
        mu = jnp.sum(jnp.sum(y, axis=0), axis=-1, keepdims=True) * (1.0 / C)
        yc = y - mu[None]
        var = jnp.sum(jnp.sum(yc * yc, axis=0), axis=-1, keepdims=True) * (1.0 / C)
        rs = lax.rsqrt(var + LN_EPS)
        for sl in range(ns):
            ls = slice(sl * V7X_LANES, (sl + 1) * V7X_LANES)
            yn = yc[sl] * rs * g_ref[:, ls] + b_ref[:, ls]
            o_ref[rows, ls] = (_silu(yn) * sz_ref[rows, ls]).astype(BF16)
        return carry

    lax.fori_loop(0, ts // NORM_ROWS, norm_body, 0)


def _conv_norm(v, sz, w_dw, b_dw, g_cn, b_cn, B, S):
    T, C = v.shape
    taps = w_dw.shape[0]
    ns = C // V7X_LANES
    assert taps - 1 <= CONV_HALO and ns % 2 == 0
    ts = _tile(S, 256)
    assert ts % CONV_HALO == 0 and ts % CONV_ROWS == 0 and ts % NORM_ROWS == 0
    per_b = S // ts
    hb = ts // CONV_HALO
    v3 = v.reshape(B, S, C)
    w_slab = w_dw.reshape(taps, ns, V7X_LANES).transpose(1, 0, 2)
    b_slab = b_dw.reshape(ns, 1, V7X_LANES)
    kern = functools.partial(_conv_kernel, taps=taps, ts=ts, C=C)
    row = lambda a: a.reshape(1, C)
    return pl.pallas_call(
        kern,
        grid=(B, per_b),
        in_specs=[
            pl.BlockSpec((1, CONV_HALO, C), lambda b, s: (b, jnp.maximum(s * hb - 1, 0), 0)),
            pl.BlockSpec((1, ts, C), lambda b, s: (b, s, 0)),
            pl.BlockSpec((ts, C), lambda b, s: (b * per_b + s, 0)),
            pl.BlockSpec((ns, taps, V7X_LANES), lambda b, s: (0, 0, 0)),
            pl.BlockSpec((ns, 1, V7X_LANES), lambda b, s: (0, 0, 0)),
            pl.BlockSpec((1, C), lambda b, s: (0, 0)),
            pl.BlockSpec((1, C), lambda b, s: (0, 0)),
        ],
        out_specs=pl.BlockSpec((ts, C), lambda b, s: (b * per_b + s, 0)),
        out_shape=jax.ShapeDtypeStruct((T, C), BF16),
        scratch_shapes=[pltpu.VMEM((ns // 2, 2 * (ts + CONV_HALO), V7X_LANES), F32),
                        pltpu.VMEM((ns, ts, V7X_LANES), F32)],
        compiler_params=_params("parallel", "parallel"),
        name="conv_norm",
    )(v3, v3, sz, w_slab, b_slab, row(g_cn), row(b_cn))


OUT_NORM_TK = 4096


def _out_kernel(act_ref, w_ref, x_ref, mod_ref, g_ref, b_ref, o_ref, y_ref,
                *, alpha, nk, nj, tn, D):
    kk = pl.program_id(1)
    j = pl.program_id(2)
    out = _dot(act_ref[...], w_ref[...])

    def residual(total):
        gate = 1.0 + mod_ref[0, 2:3, :]
        y_ref[j] = alpha * x_ref[...] + gate * total

    if nk == 1:
        residual(out)
    else:
        @pl.when(kk == 0)
        def _():
            y_ref[j] = out

        @pl.when(jnp.logical_and(kk > 0, kk < nk - 1))
        def _():
            y_ref[j] += out

        @pl.when(kk == nk - 1)
        def _():
            residual(y_ref[j] + out)

    @pl.when(jnp.logical_and(kk == nk - 1, j == nj - 1))
    def _():
        tot = jnp.sum(y_ref[0], axis=-1, keepdims=True)
        for t in range(1, nj):
            tot = tot + jnp.sum(y_ref[t], axis=-1, keepdims=True)
        mu = tot * (1.0 / D)
        sq = jnp.zeros_like(mu)
        for t in range(nj):
            d = y_ref[t] - mu
            sq = sq + jnp.sum(d * d, axis=-1, keepdims=True)
        rs = lax.rsqrt(sq * (1.0 / D) + LN_EPS)
        for t in range(nj):
            cs = slice(t * tn, (t + 1) * tn)
            o_ref[:, cs] = (y_ref[t] - mu) * rs * g_ref[:, cs] + b_ref[:, cs]


def _out_norm(act, w_out, x2, mod, ln_g, ln_b, alpha, S):
    T, K = act.shape
    D = w_out.shape[1]
    tm = _tile(S, 512)
    tn = _tile(D, 512)
    tk = _tile(K, OUT_NORM_TK)
    nj = D // tn
    nk = K // tk
    per_b = S // tm
    kern = functools.partial(_out_kernel, alpha=alpha, nk=nk, nj=nj, tn=tn, D=D)
    return pl.pallas_call(
        kern,
        grid=(T // tm, nk, nj),
        in_specs=[
            pl.BlockSpec((tm, tk), lambda i, k, j: (i, k)),
            pl.BlockSpec((tk, tn), lambda i, k, j: (k, j)),
            pl.BlockSpec((tm, tn), lambda i, k, j: (i, j)),
            pl.BlockSpec((1, 3, tn), lambda i, k, j: (i // per_b, 0, j)),
            pl.BlockSpec((1, D), lambda i, k, j: (0, 0)),
            pl.BlockSpec((1, D), lambda i, k, j: (0, 0)),
        ],
        out_specs=pl.BlockSpec((tm, D), lambda i, k, j: (i, 0)),
        out_shape=jax.ShapeDtypeStruct((T, D), F32),
        scratch_shapes=[pltpu.VMEM((nj, tm, tn), F32)],
        compiler_params=_params("parallel", "arbitrary", "arbitrary"),
        name="out_norm",
    )(act, w_out, x2, mod, ln_g.reshape(1, D), ln_b.reshape(1, D))


def _kv_kernel(x_ref, wa_ref, g_ref, cs_ref, wk_ref, wvt_ref, k_ref, vt_ref, ckv_ref, kr_ref,
               *, R, hb):
    j = pl.program_id(1)

    @pl.when(j == 0)
    def _():
        kva = _dot(x_ref[...].astype(BF16), wa_ref[...])
        ckv = kva[:, :R]
        ms = jnp.mean(ckv * ckv, axis=-1, keepdims=True)
        ckv_ref[...] = (ckv * lax.rsqrt(ms + RMS_EPS) * g_ref[...]).astype(BF16)
        u = kva[:, R:] * cs_ref[...]
        kr_ref[...] = (u[:, :QK_ROPE_DIM] + u[:, QK_ROPE_DIM:]).astype(BF16)

    ckv = ckv_ref[...]
    kn = _dot(ckv, wk_ref[...])
    vt = lax.dot_general(wvt_ref[...], ckv, (((1,), (1,)), ((), ())), preferred_element_type=F32)
    for h in range(hb):
        k_ref[0, h, :, 0:QK_NOPE_DIM] = kn[:, h * QK_NOPE_DIM:(h + 1) * QK_NOPE_DIM].astype(BF16)
        k_ref[0, h, :, QK_NOPE_DIM:] = kr_ref[...]
        vt_ref[0, h] = vt[h * V_HEAD_DIM:(h + 1) * V_HEAD_DIM, :].astype(BF16)


def _kv_proj(x2, w_kva_ext, g_kv, cs_tab, w_k, w_vt, B, S, H):
    T, D = x2.shape
    R = g_kv.shape[0]
    tm = _tile(S, 512)
    hb = _tile(H, 8)
    per_b = S // tm
    qk = QK_NOPE_DIM + QK_ROPE_DIM
    kern = functools.partial(_kv_kernel, R=R, hb=hb)
    return pl.pallas_call(
        kern,
        grid=(T // tm, H // hb),
        in_specs=[
            pl.BlockSpec((tm, D), lambda i, j: (i, 0)),
            pl.BlockSpec((D, R + 2 * QK_ROPE_DIM), lambda i, j: (0, 0)),
            pl.BlockSpec((1, R), lambda i, j: (0, 0)),
            pl.BlockSpec((tm, 2 * QK_ROPE_DIM), lambda i, j: (i % per_b, 0)),
            pl.BlockSpec((R, hb * QK_NOPE_DIM), lambda i, j: (0, j)),
            pl.BlockSpec((hb * V_HEAD_DIM, R), lambda i, j: (j, 0)),
        ],
        out_specs=[
            pl.BlockSpec((1, hb, tm, qk), lambda i, j: (i // per_b, j, i % per_b, 0)),
            pl.BlockSpec((1, hb, V_HEAD_DIM, tm), lambda i, j: (i // per_b, j, 0, i % per_b)),
        ],
        out_shape=[jax.ShapeDtypeStruct((B, H, S, qk), BF16),
                   jax.ShapeDtypeStruct((B, H, V_HEAD_DIM, S), BF16)],
        scratch_shapes=[pltpu.VMEM((tm, R), BF16), pltpu.VMEM((tm, QK_ROPE_DIM), BF16)],
        compiler_params=_params("parallel", "arbitrary"),
        name="kv_proj",
    )(x2, w_kva_ext, g_kv.reshape(1, R), cs_tab, w_k, w_vt)


def _q_kernel(h_ref, wc_ref, g_ref, cos_ref, sin_ref, wqt_ref, q_ref, cq_ref, *, hb, scale):
    j = pl.program_id(1)

    @pl.when(j == 0)
    def _():
        cq = _dot(h_ref[...], wc_ref[...])
        ms = jnp.mean(cq * cq, axis=-1, keepdims=True)
        cq_ref[...] = (cq * lax.rsqrt(ms + RMS_EPS) * g_ref[...]).astype(BF16)

    qt = lax.dot_general(wqt_ref[...], cq_ref[...], (((1,), (1,)), ((), ())),
                         preferred_element_type=F32) * scale
    qk = QK_NOPE_DIM + QK_ROPE_DIM
    half = QK_ROPE_DIM // 2
    cos = cos_ref[...]
    sin = sin_ref[...]
    for h in range(hb):
        r0 = h * qk
        x1 = qt[r0 + QK_NOPE_DIM:r0 + QK_NOPE_DIM + half, :]
        x2 = qt[r0 + QK_NOPE_DIM + half:r0 + qk, :]
        q_ref[0, h, 0:QK_NOPE_DIM, :] = qt[r0:r0 + QK_NOPE_DIM, :].astype(BF16)
        q_ref[0, h, QK_NOPE_DIM:QK_NOPE_DIM + half, :] = (x1 * cos - x2 * sin).astype(BF16)
        q_ref[0, h, QK_NOPE_DIM + half:qk, :] = (x1 * sin + x2 * cos).astype(BF16)


def _q_proj(h, w_cq, g_q, cos_t, sin_t, w_qt, B, S, H, scale):
    T, D = h.shape
    R = g_q.shape[0]
    tm = _tile(S, 512)
    hb = _tile(H, 8)
    per_b = S // tm
    qk = QK_NOPE_DIM + QK_ROPE_DIM
    half = QK_ROPE_DIM // 2
    kern = functools.partial(_q_kernel, hb=hb, scale=scale)
    return pl.pallas_call(
        kern,
        grid=(T // tm, H // hb),
        in_specs=[
            pl.BlockSpec((tm, D), lambda i, j: (i, 0)),
            pl.BlockSpec((D, R), lambda i, j: (0, 0)),
            pl.BlockSpec((1, R), lambda i, j: (0, 0)),
            pl.BlockSpec((half, tm), lambda i, j: (0, i % per_b)),
            pl.BlockSpec((half, tm), lambda i, j: (0, i % per_b)),
            pl.BlockSpec((hb * qk, R), lambda i, j: (j, 0)),
        ],
        out_specs=pl.BlockSpec((1, hb, qk, tm), lambda i, j: (i // per_b, j, 0, i % per_b)),
        out_shape=jax.ShapeDtypeStruct((B, H, qk, S), BF16),
        scratch_shapes=[pltpu.VMEM((tm, R), BF16)],
        compiler_params=_params("parallel", "arbitrary"),
        name="q_proj",
    )(h, w_cq, g_q.reshape(1, R), cos_t, sin_t, w_qt)


def _silu_proj_kernel(h_ref, w_ref, o_ref):
    o_ref[...] = _silu(_dot(h_ref[...], w_ref[...]))


def _silu_proj(h, w_z):
    T, D = h.shape
    N = w_z.shape[1]
    tm = _tile(T, 1024)
    tn = _tile(N, 512)
    return pl.pallas_call(
        _silu_proj_kernel,
        grid=(T // tm, N // tn),
        in_specs=[
            pl.BlockSpec((tm, D), lambda i, j: (i, 0)),
            pl.BlockSpec((D, tn), lambda i, j: (0, j)),
        ],
        out_specs=pl.BlockSpec((tm, tn), lambda i, j: (i, j)),
        out_shape=jax.ShapeDtypeStruct((T, N), F32),
        compiler_params=_params("parallel", "arbitrary"),
        name="silu_proj",
    )(h, w_z)


ATTN_HEADS_PER_STEP = 4
ATTN_EXP_ROWS = 16


def _attn_kernel(q_ref, k_ref, vt_ref, sz_ref, bias_ref, o_ref,
                 m_ref, l_ref, acc_ref, s_ref, p_ref, mx_ref, al_ref, *, tq, tk, G):
    qi = pl.program_id(2)
    m_ref[...] = jnp.full(m_ref.shape, -jnp.inf, F32)
    l_ref[...] = jnp.zeros(l_ref.shape, F32)
    acc_ref[...] = jnp.zeros(acc_ref.shape, F32)
    n = G * (qi + 1)
    lg = G.bit_length() - 1
    ch = ATTN_EXP_ROWS

    def item(it):
        g = jnp.bitwise_and(it, G - 1)
        ki = jnp.right_shift(it, lg)
        return g, ki, pl.multiple_of(ki * tk, tk)

    def scores(it, slot):
        g, ki, k0 = item(it)
        s = _dot(k_ref[0, g, pl.ds(k0, tk), :], q_ref[0, g])
        s = s + bias_ref[(ki == qi).astype(jnp.int32)]
        s_ref[slot] = s
        mx_ref[slot] = jnp.max(s.reshape(tk // SUBLANES, SUBLANES, tq), axis=0)

    def softmax(it, slot):
        g, _, _ = item(it)
        m_prev = m_ref[g]
        m_new = jnp.maximum(m_prev, jnp.max(mx_ref[slot], axis=0, keepdims=True))
        alpha = jnp.exp2(m_prev - m_new)
        mb = jnp.broadcast_to(m_new, (ch, tq))
        ls = None
        for c in range(tk // ch):
            p = jnp.exp2(s_ref[slot, c * ch:(c + 1) * ch, :] - mb)
            ls = p if ls is None else ls + p
            p_ref[slot, c * ch:(c + 1) * ch, :] = p.astype(BF16)
        l_ref[g] = alpha * l_ref[g] + jnp.sum(ls, axis=0, keepdims=True)
        m_ref[g] = m_new
        al_ref[slot] = alpha

    def accumulate(it, slot):
        g, _, k0 = item(it)
        pv = _dot(vt_ref[0, g, :, pl.ds(k0, tk)], p_ref[slot])
        acc_ref[g] = al_ref[slot] * acc_ref[g] + pv

    scores(0, 0)
    scores(1, 1)
    softmax(0, 0)

    def steady(t, slot):
        softmax(t - 1, 1 - slot)
        scores(t, slot)
        accumulate(t - 2, slot)

    def body(j, carry):
        steady(2 * j, 0)
        steady(2 * j + 1, 1)
        return carry

    lax.fori_loop(1, n // 2, body, 0)
    accumulate(n - 2, 0)
    softmax(n - 1, 1)
    accumulate(n - 1, 1)

    for g in range(G):
        cs = slice(g * V_HEAD_DIM, (g + 1) * V_HEAD_DIM)
        o = (acc_ref[g] / l_ref[g]).T
        o_ref[:, cs] = (o * sz_ref[:, cs]).astype(BF16)


def _attention(q, k, vt, sz, B, S, H):
    tq = tk = _tile(S, 512)
    G = _tile(H, ATTN_HEADS_PER_STEP)
    assert G >= 2 and G & (G - 1) == 0
    nq = S // tq
    qk = k.shape[-1]
    r = lax.broadcasted_iota(jnp.int32, (tk, tq), 0)
    c = lax.broadcasted_iota(jnp.int32, (tk, tq), 1)
    bias = jnp.stack([jnp.zeros((tk, tq), F32), jnp.where(c >= r, 0.0, -jnp.inf).astype(F32)])
    kern = functools.partial(_attn_kernel, tq=tq, tk=tk, G=G)
    return pl.pallas_call(
        kern,
        grid=(B, H // G, nq),
        in_specs=[
            pl.BlockSpec((1, G, qk, tq), lambda b, h, i: (b, h, 0, i)),
            pl.BlockSpec((1, G, S, qk), lambda b, h, i: (b, h, 0, 0)),
            pl.BlockSpec((1, G, V_HEAD_DIM, S), lambda b, h, i: (b, h, 0, 0)),
            pl.BlockSpec((tq, G * V_HEAD_DIM), lambda b, h, i: (b * nq + i, h)),
            pl.BlockSpec((2, tk, tq), lambda b, h, i: (0, 0, 0)),
        ],
        out_specs=pl.BlockSpec((tq, G * V_HEAD_DIM), lambda b, h, i: (b * nq + i, h)),
        out_shape=jax.ShapeDtypeStruct((B * S, H * V_HEAD_DIM), BF16),
        scratch_shapes=[pltpu.VMEM((G, 1, tq), F32), pltpu.VMEM((G, 1, tq), F32),
                        pltpu.VMEM((G, V_HEAD_DIM, tq), F32),
                        pltpu.VMEM((2, tk, tq), F32), pltpu.VMEM((2, tk, tq), BF16),
                        pltpu.VMEM((2, SUBLANES, tq), F32), pltpu.VMEM((2, 1, tq), F32)],
        compiler_params=_params("parallel", "parallel", "arbitrary"),
        name="attention",
    )(q, k, vt, sz, bias)


def _rope_tables(S):
    half = QK_ROPE_DIM // 2
    inv_freq = ROPE_BASE ** (-jnp.arange(half, dtype=F32) / half)
    ang = jnp.arange(S, dtype=jnp.int32).astype(F32)[:, None] * inv_freq[None, :]
    return jnp.cos(ang), jnp.sin(ang)


def kernel(x, c, w_ada, b_ada, ln_g, ln_b, a_w_in, a_w_dw, a_b_dw, a_norm_g, a_norm_b, a_w_out,
           b_w_in, b_q_norm_g, b_w_qb, b_w_out, kv_w_a, kv_norm_g, kv_w_b):
    B, S, D = x.shape
    depth = w_ada.shape[0]
    n_a = a_w_in.shape[0]
    alpha = (2.0 * depth) ** 0.25
    H = b_w_out.shape[1] // V_HEAD_DIM
    q_rank = b_q_norm_g.shape[1]
    kv_rank = kv_norm_g.shape[0]
    qk = QK_NOPE_DIM + QK_ROPE_DIM
    half = QK_ROPE_DIM // 2

    mod = _ada_mod(c, w_ada, b_ada)
    cos, sin = _rope_tables(S)
    k_tab = jnp.concatenate([cos, cos, -sin, sin], axis=1)

    x2 = x.reshape(B * S, D)
    kv = None
    for layer in range(depth):
        if layer == n_a:
            rope_w = kv_w_a[:, kv_rank:]
            w_kva_ext = jnp.concatenate(
                [kv_w_a, rope_w[:, half:], rope_w[:, :half]], axis=1).astype(BF16)
            w_kvb = kv_w_b.reshape(kv_rank, H, QK_NOPE_DIM + V_HEAD_DIM)
            w_k = w_kvb[:, :, :QK_NOPE_DIM].reshape(kv_rank, H * QK_NOPE_DIM).astype(BF16)
            w_vt = w_kvb[:, :, QK_NOPE_DIM:].reshape(kv_rank, H * V_HEAD_DIM).T.astype(BF16)
            kv = _kv_proj(x2, w_kva_ext, kv_norm_g, k_tab, w_k, w_vt, B, S, H)
        h = _ln_modulate(x2, mod[layer], S)
        if layer < n_a:
            i = layer
            v, sz = _glu_proj(h, a_w_in[i].astype(BF16))
            act = _conv_norm(v, sz, a_w_dw[i], a_b_dw[i], a_norm_g[i], a_norm_b[i], B, S)
            w_out = a_w_out[i]
        else:
            j = layer - n_a
            w_in = b_w_in[j]
            q = _q_proj(h, w_in[:, :q_rank].astype(BF16), b_q_norm_g[j], cos.T, sin.T,
                        b_w_qb[j].T.astype(BF16), B, S, H, qk ** -0.5 * math.log2(math.e))
            sz = _silu_proj(h, w_in[:, q_rank:].astype(BF16))
            act = _attention(q, kv[0], kv[1], sz, B, S, H)
            w_out = b_w_out[j]
        x2 = _out_norm(act, w_out.astype(BF16), x2, mod[layer], ln_g[layer], ln_b[layer], alpha, S)
    return x2.reshape(B, S, D)
```

```python
import functools
import math

import jax
import jax.numpy as jnp
from jax import lax
from jax.experimental import pallas as pl
from jax.experimental.pallas import tpu as pltpu

QK_NOPE_DIM = 128
QK_ROPE_DIM = 64
V_HEAD_DIM = 128
ROPE_BASE = 10000.0
LN_EPS = 1e-5
RMS_EPS = 1e-6

V7X_LANES = 128
V7X_VMEM_BYTES = 64 * 1024 * 1024
VMEM_LIMIT_BYTES = 56 * 1024 * 1024

F32 = jnp.float32
BF16 = jnp.bfloat16


def _params(*semantics):
    return pltpu.CompilerParams(dimension_semantics=semantics,
                                vmem_limit_bytes=VMEM_LIMIT_BYTES)


def _tile(dim, want):
    t = min(dim, want)
    while dim % t:
        t //= 2
    return t


def _dot(a, b):
    return jnp.dot(a, b, preferred_element_type=F32)


def _silu(x):
    return x * jax.nn.sigmoid(x)


def _ada_kernel(c_ref, w_ref, b_ref, o_ref):
    sc = _silu(c_ref[...]).astype(BF16)
    o_ref[0] = _dot(sc, w_ref[0].astype(BF16)) + b_ref[0]


def _ada_mod(c, w_ada, b_ada):
    L, D, N = w_ada.shape
    B = c.shape[0]
    Bp = max(8, B)
    cp = jnp.pad(c, ((0, Bp - B), (0, 0)))
    tn = _tile(N, 512)
    out = pl.pallas_call(
        _ada_kernel,
        grid=(L, N // tn),
        in_specs=[
            pl.BlockSpec((Bp, D), lambda l, j: (0, 0)),
            pl.BlockSpec((1, D, tn), lambda l, j: (l, 0, j)),
            pl.BlockSpec((1, 1, tn), lambda l, j: (l, 0, j)),
        ],
        out_specs=pl.BlockSpec((1, Bp, tn), lambda l, j: (l, 0, j)),
        out_shape=jax.ShapeDtypeStruct((L, Bp, N), F32),
        compiler_params=_params("parallel", "parallel"),
        name="ada_mod",
    )(cp, w_ada, b_ada.reshape(L, 1, N))
    return out[:, :B].reshape(L, B, 3, D)


def _ln_mod_kernel(x_ref, mod_ref, h_ref):
    x = x_ref[...]
    mu = jnp.mean(x, axis=-1, keepdims=True)
    xc = x - mu
    var = jnp.mean(xc * xc, axis=-1, keepdims=True)
    y = xc * lax.rsqrt(var + LN_EPS)
    shift = mod_ref[0, 0:1, :]
    scale = mod_ref[0, 1:2, :]
    h_ref[...] = (y * (1.0 + scale) + shift).astype(BF16)


def _ln_modulate(x2, mod, S):
    T, D = x2.shape
    tm = _tile(S, 256)
    per_b = S // tm
    return pl.pallas_call(
        _ln_mod_kernel,
        grid=(T // tm,),
        in_specs=[
            pl.BlockSpec((tm, D), lambda i: (i, 0)),
            pl.BlockSpec((1, 3, D), lambda i: (i // per_b, 0, 0)),
        ],
        out_specs=pl.BlockSpec((tm, D), lambda i: (i, 0)),
        out_shape=jax.ShapeDtypeStruct((T, D), BF16),
        compiler_params=_params("parallel"),
        name="ln_modulate",
    )(x2, mod)


def _glu_kernel(h_ref, wa_ref, wg_ref, wz_ref, v_ref, sz_ref):
    h = h_ref[...]
    a = _dot(h, wa_ref[...])
    g = _dot(h, wg_ref[...])
    v_ref[...] = a * jax.nn.sigmoid(g)
    z = _dot(h, wz_ref[...])
    sz_ref[...] = _silu(z)


def _glu_proj(h, w_in):
    T, D = h.shape
    C = w_in.shape[1] // 3
    tm = _tile(T, 1024)
    tn = _tile(C, 256)
    nj = C // tn
    return pl.pallas_call(
        _glu_kernel,
        grid=(T // tm, nj),
        in_specs=[
            pl.BlockSpec((tm, D), lambda i, j: (i, 0)),
            pl.BlockSpec((D, tn), lambda i, j: (0, j)),
            pl.BlockSpec((D, tn), lambda i, j: (0, j + nj)),
            pl.BlockSpec((D, tn), lambda i, j: (0, j + 2 * nj)),
        ],
        out_specs=[
            pl.BlockSpec((tm, tn), lambda i, j: (i, j)),
            pl.BlockSpec((tm, tn), lambda i, j: (i, j)),
        ],
        out_shape=[jax.ShapeDtypeStruct((T, C), F32), jax.ShapeDtypeStruct((T, C), F32)],
        compiler_params=_params("parallel", "arbitrary"),
        name="glu_proj",
    )(h, w_in, w_in, w_in)


CONV_HALO = 32
CONV_ROWS = 64
NORM_ROWS = 16
SUBLANES = 8


def _conv_kernel(vh_ref, vm_ref, sz_ref, w_ref, bdw_ref, g_ref, b_ref, o_ref,
                 slab_ref, cv_ref, mu_ref, rs_ref,
                 *, taps, ts, C):
    s = pl.program_id(1)
    ns = C // V7X_LANES
    lead = CONV_HALO - (taps - 1)
    for sl in range(ns):
        p, par = divmod(sl, 2)
        ls = slice(sl * V7X_LANES, (sl + 1) * V7X_LANES)
        halo = vh_ref[0, :, ls]
        slab_ref[p, pl.ds(par, CONV_HALO, stride=2), :] = jnp.where(s > 0, halo, jnp.zeros_like(halo))
        slab_ref[p, pl.ds(2 * CONV_HALO + par, ts, stride=2), :] = vm_ref[0, :, ls]

    nacc = CONV_ROWS // SUBLANES

    def slab_body(sl, carry):
        p = sl // 2
        par = sl % 2
        wk = [jnp.broadcast_to(w_ref[sl, k:k + 1, :], (SUBLANES, V7X_LANES)) for k in range(taps)]
        bias = bdw_ref[sl]

        def row_body(rb, c2):
            r0 = rb * CONV_ROWS
            accs = [None] * nacc
            for o in range(lead, lead + taps + CONV_ROWS - SUBLANES):
                x = slab_ref[p, pl.ds(2 * (r0 + o) + par, SUBLANES, stride=2), :]
                for i in range(nacc):
                    k = o - lead - SUBLANES * i
                    if 0 <= k < taps:
                        t = x * wk[k]
                        accs[i] = t if accs[i] is None else accs[i] + t
            for i in range(nacc):
                rows = pl.ds(pl.multiple_of(r0 + SUBLANES * i, SUBLANES), SUBLANES)
                cv_ref[sl, rows, :] = accs[i] + bias
            return c2

        lax.fori_loop(0, ts // CONV_ROWS, row_body, 0)
        return carry

    lax.fori_loop(0, ns, slab_body, 0)

    def stats_body(rb, carry):
        rows = pl.ds(pl.multiple_of(rb * CONV_ROWS, CONV_ROWS), CONV_ROWS)
        tot = cv_ref[0, rows, :]
        for sl in range(1, ns):
            tot = tot + cv_ref[sl, rows, :]
        mu = jnp.broadcast_to(jnp.sum(tot, axis=-1, keepdims=True) * (1.0 / C),
                              (CONV_ROWS, V7X_LANES))
        sq = None
        for sl in range(ns):
            d = cv_ref[sl, rows, :] - mu
            sq = d * d if sq is None else sq + d * d
        var = jnp.sum(sq, axis=-1, keepdims=True) * (1.0 / C)
        mu_ref[rows, :] = mu
        rs_ref[rows, :] = jnp.broadcast_to(lax.rsqrt(var + LN_EPS), (CONV_ROWS, V7X_LANES))
        return carry

    lax.fori_loop(0, ts // CONV_ROWS, stats_body, 0)

    def norm_body(rg, carry):
        rows = pl.ds(pl.multiple_of(rg * NORM_ROWS, NORM_ROWS), NORM_ROWS)
        mu = mu_ref[rows, :]
        rs = rs_ref[rows, :]
        for sl in range(ns):
            ls = slice(sl * V7X_LANES, (sl + 1) * V7X_LANES)
            yn = (cv_ref[sl, rows, :] - mu) * rs * g_ref[:, ls] + b_ref[:, ls]
            o_ref[rows, ls] = (_silu(yn) * sz_ref[rows, ls]).astype(BF16)
        return carry

    lax.fori_loop(0, ts // NORM_ROWS, norm_body, 0)


def _conv_norm(v, sz, w_dw, b_dw, g_cn, b_cn, B, S):
    T, C = v.shape
    taps = w_dw.shape[0]
    ns = C // V7X_LANES
    assert taps - 1 <= CONV_HALO and ns % 2 == 0
    ts = _tile(S, 256)
    assert ts % CONV_HALO == 0 and ts % CONV_ROWS == 0 and ts % NORM_ROWS == 0
    per_b = S // ts
    hb = ts // CONV_HALO
    v3 = v.reshape(B, S, C)
    w_slab = w_dw.reshape(taps, ns, V7X_LANES).transpose(1, 0, 2)
    b_slab = b_dw.reshape(ns, 1, V7X_LANES)
    kern = functools.partial(_conv_kernel, taps=taps, ts=ts, C=C)
    row = lambda a: a.reshape(1, C)
    return pl.pallas_call(
        kern,
        grid=(B, per_b),
        in_specs=[
            pl.BlockSpec((1, CONV_HALO, C), lambda b, s: (b, jnp.maximum(s * hb - 1, 0), 0)),
            pl.BlockSpec((1, ts, C), lambda b, s: (b, s, 0)),
            pl.BlockSpec((ts, C), lambda b, s: (b * per_b + s, 0)),
            pl.BlockSpec((ns, taps, V7X_LANES), lambda b, s: (0, 0, 0)),
            pl.BlockSpec((ns, 1, V7X_LANES), lambda b, s: (0, 0, 0)),
            pl.BlockSpec((1, C), lambda b, s: (0, 0)),
            pl.BlockSpec((1, C), lambda b, s: (0, 0)),
        ],
        out_specs=pl.BlockSpec((ts, C), lambda b, s: (b * per_b + s, 0)),
        out_shape=jax.ShapeDtypeStruct((T, C), BF16),
        scratch_shapes=[pltpu.VMEM((ns // 2, 2 * (ts + CONV_HALO), V7X_LANES), F32),
                        pltpu.VMEM((ns, ts, V7X_LANES), F32),
                        pltpu.VMEM((ts, V7X_LANES), F32), pltpu.VMEM((ts, V7X_LANES), F32)],
        compiler_params=_params("parallel", "parallel"),
        name="conv_norm",
    )(v3, v3, sz, w_slab, b_slab, row(g_cn), row(b_cn))


OUT_NORM_TK = 4096


def _out_kernel(act_ref, w_ref, x_ref, mod_ref, g_ref, b_ref, o_ref, y_ref,
                *, alpha, nk, nj, tn, D):
    kk = pl.program_id(1)
    j = pl.program_id(2)
    out = _dot(act_ref[...], w_ref[...])

    def residual(total):
        gate = 1.0 + mod_ref[0, 2:3, :]
        y_ref[j] = alpha * x_ref[...] + gate * total

    if nk == 1:
        residual(out)
    else:
        @pl.when(kk == 0)
        def _():
            y_ref[j] = out

        @pl.when(jnp.logical_and(kk > 0, kk < nk - 1))
        def _():
            y_ref[j] += out

        @pl.when(kk == nk - 1)
        def _():
            residual(y_ref[j] + out)

    @pl.when(jnp.logical_and(kk == nk - 1, j == nj - 1))
    def _():
        tot = jnp.sum(y_ref[0], axis=-1, keepdims=True)
        for t in range(1, nj):
            tot = tot + jnp.sum(y_ref[t], axis=-1, keepdims=True)
        mu = tot * (1.0 / D)
        sq = jnp.zeros_like(mu)
        for t in range(nj):
            d = y_ref[t] - mu
            sq = sq + jnp.sum(d * d, axis=-1, keepdims=True)
        rs = lax.rsqrt(sq * (1.0 / D) + LN_EPS)
        for t in range(nj):
            cs = slice(t * tn, (t + 1) * tn)
            o_ref[:, cs] = (y_ref[t] - mu) * rs * g_ref[:, cs] + b_ref[:, cs]


def _out_norm(act, w_out, x2, mod, ln_g, ln_b, alpha, S):
    T, K = act.shape
    D = w_out.shape[1]
    tm = _tile(S, 512)
    tn = _tile(D, 512)
    tk = _tile(K, OUT_NORM_TK)
    nj = D // tn
    nk = K // tk
    per_b = S // tm
    kern = functools.partial(_out_kernel, alpha=alpha, nk=nk, nj=nj, tn=tn, D=D)
    return pl.pallas_call(
        kern,
        grid=(T // tm, nk, nj),
        in_specs=[
            pl.BlockSpec((tm, tk), lambda i, k, j: (i, k)),
            pl.BlockSpec((tk, tn), lambda i, k, j: (k, j)),
            pl.BlockSpec((tm, tn), lambda i, k, j: (i, j)),
            pl.BlockSpec((1, 3, tn), lambda i, k, j: (i // per_b, 0, j)),
            pl.BlockSpec((1, D), lambda i, k, j: (0, 0)),
            pl.BlockSpec((1, D), lambda i, k, j: (0, 0)),
        ],
        out_specs=pl.BlockSpec((tm, D), lambda i, k, j: (i, 0)),
        out_shape=jax.ShapeDtypeStruct((T, D), F32),
        scratch_shapes=[pltpu.VMEM((nj, tm, tn), F32)],
        compiler_params=_params("parallel", "arbitrary", "arbitrary"),
        name="out_norm",
    )(act, w_out, x2, mod, ln_g.reshape(1, D), ln_b.reshape(1, D))


K_ROW_LANES = 2 * V7X_LANES


def _kv_kernel(x_ref, wa_ref, g_ref, cs_ref, wk_ref, wvt_ref, k_ref, vt_ref, ckv_ref, kr_ref,
               *, R, hb):
    j = pl.program_id(1)

    @pl.when(j == 0)
    def _():
        kva = _dot(x_ref[...].astype(BF16), wa_ref[...])
        ckv = kva[:, :R]
        ms = jnp.mean(ckv * ckv, axis=-1, keepdims=True)
        ckv_ref[...] = (ckv * lax.rsqrt(ms + RMS_EPS) * g_ref[...]).astype(BF16)
        u = kva[:, R:] * cs_ref[...]
        kr_ref[...] = (u[:, :QK_ROPE_DIM] + u[:, QK_ROPE_DIM:]).astype(BF16)

    qk = QK_NOPE_DIM + QK_ROPE_DIM
    ckv = ckv_ref[...]
    kn = _dot(ckv, wk_ref[...])
    vt = lax.dot_general(wvt_ref[...], ckv, (((1,), (1,)), ((), ())), preferred_element_type=F32)
    for h in range(hb):
        k_ref[0, h, :, 0:QK_NOPE_DIM] = kn[:, h * QK_NOPE_DIM:(h + 1) * QK_NOPE_DIM].astype(BF16)
        k_ref[0, h, :, QK_NOPE_DIM:qk] = kr_ref[...]
        k_ref[0, h, :, qk:] = jnp.zeros((k_ref.shape[2], K_ROW_LANES - qk), BF16)
        vt_ref[0, h] = vt[h * V_HEAD_DIM:(h + 1) * V_HEAD_DIM, :].astype(BF16)


def _kv_proj(x2, w_kva_ext, g_kv, cs_tab, w_k, w_vt, B, S, H):
    T, D = x2.shape
    R = g_kv.shape[0]
    tm = _tile(S, 512)
    hb = _tile(H, 8)
    per_b = S // tm
    qk = QK_NOPE_DIM + QK_ROPE_DIM
    kern = functools.partial(_kv_kernel, R=R, hb=hb)
    return pl.pallas_call(
        kern,
        grid=(T // tm, H // hb),
        in_specs=[
            pl.BlockSpec((tm, D), lambda i, j: (i, 0)),
            pl.BlockSpec((D, R + 2 * QK_ROPE_DIM), lambda i, j: (0, 0)),
            pl.BlockSpec((1, R), lambda i, j: (0, 0)),
            pl.BlockSpec((tm, 2 * QK_ROPE_DIM), lambda i, j: (i % per_b, 0)),
            pl.BlockSpec((R, hb * QK_NOPE_DIM), lambda i, j: (0, j)),
            pl.BlockSpec((hb * V_HEAD_DIM, R), lambda i, j: (j, 0)),
        ],
        out_specs=[
            pl.BlockSpec((1, hb, tm, K_ROW_LANES), lambda i, j: (i // per_b, j, i % per_b, 0)),
            pl.BlockSpec((1, hb, V_HEAD_DIM, tm), lambda i, j: (i // per_b, j, 0, i % per_b)),
        ],
        out_shape=[jax.ShapeDtypeStruct((B, H, S, K_ROW_LANES), BF16),
                   jax.ShapeDtypeStruct((B, H, V_HEAD_DIM, S), BF16)],
        scratch_shapes=[pltpu.VMEM((tm, R), BF16), pltpu.VMEM((tm, QK_ROPE_DIM), BF16)],
        compiler_params=_params("parallel", "arbitrary"),
        name="kv_proj",
    )(x2, w_kva_ext, g_kv.reshape(1, R), cs_tab, w_k, w_vt)


def _q_kernel(h_ref, wc_ref, g_ref, cos_ref, sin_ref, wqt_ref, q_ref, cq_ref, *, hb, scale):
    j = pl.program_id(1)

    @pl.when(j == 0)
    def _():
        cq = _dot(h_ref[...], wc_ref[...])
        ms = jnp.mean(cq * cq, axis=-1, keepdims=True)
        cq_ref[...] = (cq * lax.rsqrt(ms + RMS_EPS) * g_ref[...]).astype(BF16)

    qt = lax.dot_general(wqt_ref[...], cq_ref[...], (((1,), (1,)), ((), ())),
                         preferred_element_type=F32) * scale
    qk = QK_NOPE_DIM + QK_ROPE_DIM
    half = QK_ROPE_DIM // 2
    cos = cos_ref[...]
    sin = sin_ref[...]
    for h in range(hb):
        r0 = h * qk
        x1 = qt[r0 + QK_NOPE_DIM:r0 + QK_NOPE_DIM + half, :]
        x2 = qt[r0 + QK_NOPE_DIM + half:r0 + qk, :]
        q_ref[0, h, 0:QK_NOPE_DIM, :] = qt[r0:r0 + QK_NOPE_DIM, :].astype(BF16)
        q_ref[0, h, QK_NOPE_DIM:QK_NOPE_DIM + half, :] = (x1 * cos - x2 * sin).astype(BF16)
        q_ref[0, h, QK_NOPE_DIM + half:qk, :] = (x1 * sin + x2 * cos).astype(BF16)


def _q_proj(h, w_cq, g_q, cos_t, sin_t, w_qt, B, S, H, scale):
    T, D = h.shape
    R = g_q.shape[0]
    tm = _tile(S, 512)
    hb = _tile(H, 8)
    per_b = S // tm
    qk = QK_NOPE_DIM + QK_ROPE_DIM
    half = QK_ROPE_DIM // 2
    kern = functools.partial(_q_kernel, hb=hb, scale=scale)
    return pl.pallas_call(
        kern,
        grid=(T // tm, H // hb),
        in_specs=[
            pl.BlockSpec((tm, D), lambda i, j: (i, 0)),
            pl.BlockSpec((D, R), lambda i, j: (0, 0)),
            pl.BlockSpec((1, R), lambda i, j: (0, 0)),
            pl.BlockSpec((half, tm), lambda i, j: (0, i % per_b)),
            pl.BlockSpec((half, tm), lambda i, j: (0, i % per_b)),
            pl.BlockSpec((hb * qk, R), lambda i, j: (j, 0)),
        ],
        out_specs=pl.BlockSpec((1, hb, qk, tm), lambda i, j: (i // per_b, j, 0, i % per_b)),
        out_shape=jax.ShapeDtypeStruct((B, H, qk, S), BF16),
        scratch_shapes=[pltpu.VMEM((tm, R), BF16)],
        compiler_params=_params("parallel", "arbitrary"),
        name="q_proj",
    )(h, w_cq, g_q.reshape(1, R), cos_t, sin_t, w_qt)


def _silu_proj_kernel(h_ref, w_ref, o_ref):
    o_ref[...] = _silu(_dot(h_ref[...], w_ref[...]))


def _silu_proj(h, w_z):
    T, D = h.shape
    N = w_z.shape[1]
    tm = _tile(T, 1024)
    tn = _tile(N, 512)
    return pl.pallas_call(
        _silu_proj_kernel,
        grid=(T // tm, N // tn),
        in_specs=[
            pl.BlockSpec((tm, D), lambda i, j: (i, 0)),
            pl.BlockSpec((D, tn), lambda i, j: (0, j)),
        ],
        out_specs=pl.BlockSpec((tm, tn), lambda i, j: (i, j)),
        out_shape=jax.ShapeDtypeStruct((T, N), F32),
        compiler_params=_params("parallel", "arbitrary"),
        name="silu_proj",
    )(h, w_z)


ATTN_HEADS_PER_STEP = 4
ATTN_QBLOCKS_PER_STEP = 4
ATTN_EXP_ROWS = 16


def _attn_kernel(q_ref, k_ref, vt_ref, sz_ref, bias_ref, o_ref,
                 m_ref, l_ref, acc_ref, s_ref, p_ref, mb_ref, al_ref, *, tq, tk, G, QB):
    q0 = pl.program_id(2) * QB
    m_ref[...] = jnp.full(m_ref.shape, -jnp.inf, F32)
    l_ref[...] = jnp.zeros(l_ref.shape, F32)
    acc_ref[...] = jnp.zeros(acc_ref.shape, F32)
    lg = G.bit_length() - 1
    ch = ATTN_EXP_ROWS

    def first_item(qb):
        return G * (qb * (q0 + 1) + ((qb * (qb - 1)) >> 1))

    n = first_item(QB)

    def item(it):
        qb = 0
        for u in range(1, QB):
            qb = qb + (it >= first_item(u)).astype(jnp.int32)
        local = it - first_item(qb)
        g = jnp.bitwise_and(local, G - 1)
        ki = jnp.right_shift(local, lg)
        return qb, g, ki, pl.multiple_of(ki * tk, tk)

    def scores(it, slot):
        qb, g, ki, k0 = item(it)
        st = qb * G + g
        q = q_ref[0, g, :, pl.ds(pl.multiple_of(qb * tq, tq), tq)]
        s = _dot(k_ref[0, g, pl.ds(k0, tk), 0:q.shape[0]], q)
        s = s + bias_ref[(ki == q0 + qb).astype(jnp.int32)]
        s_ref[slot] = s
        mx = jnp.max(s.reshape(tk // SUBLANES, SUBLANES, tq), axis=0)
        m_prev = m_ref[st]
        m_new = jnp.maximum(m_prev, jnp.max(mx, axis=0, keepdims=True))
        m_ref[st] = m_new
        mb_ref[slot] = m_new
        al_ref[jnp.bitwise_and(it, 3)] = jnp.exp2(m_prev - m_new)

    def softmax(it, slot):
        qb, g, _, _ = item(it)
        st = qb * G + g
        mb = jnp.broadcast_to(mb_ref[slot], (ch, tq))
        ls = None
        for c in range(tk // ch):
            p = jnp.exp2(s_ref[slot, c * ch:(c + 1) * ch, :] - mb)
            ls = p if ls is None else ls + p
            p_ref[slot, c * ch:(c + 1) * ch, :] = p.astype(BF16)
        alpha = al_ref[jnp.bitwise_and(it, 3)]
        l_ref[st] = alpha * l_ref[st] + jnp.sum(ls, axis=0, keepdims=True)

    def accumulate(it, slot):
        qb, g, _, k0 = item(it)
        st = qb * G + g
        pv = _dot(vt_ref[0, g, :, pl.ds(k0, tk)], p_ref[slot])
        acc_ref[st] = al_ref[jnp.bitwise_and(it, 3)] * acc_ref[st] + pv

    scores(0, 0)
    scores(1, 1)
    softmax(0, 0)

    def steady(t, slot):
        accumulate(t - 2, slot)
        softmax(t - 1, 1 - slot)
        scores(t, slot)

    def body(j, carry):
        steady(2 * j, 0)
        steady(2 * j + 1, 1)
        return carry

    lax.fori_loop(1, n // 2, body, 0)
    accumulate(n - 2, 0)
    softmax(n - 1, 1)
    accumulate(n - 1, 1)

    for qb in range(QB):
        rows = slice(qb * tq, (qb + 1) * tq)
        for g in range(G):
            cs = slice(g * V_HEAD_DIM, (g + 1) * V_HEAD_DIM)
            o = (acc_ref[qb * G + g] / l_ref[qb * G + g]).T
            o_ref[rows, cs] = (o * sz_ref[rows, cs]).astype(BF16)


def _attention(q, k, vt, sz, B, S, H):
    tq = tk = _tile(S, 512)
    G = _tile(H, ATTN_HEADS_PER_STEP)
    assert G >= 2 and G & (G - 1) == 0
    nq = S // tq
    QB = _tile(nq, ATTN_QBLOCKS_PER_STEP)
    ns = nq // QB
    qk = q.shape[2]
    r = lax.broadcasted_iota(jnp.int32, (tk, tq), 0)
    c = lax.broadcasted_iota(jnp.int32, (tk, tq), 1)
    bias = jnp.stack([jnp.zeros((tk, tq), F32), jnp.where(c >= r, 0.0, -jnp.inf).astype(F32)])
    kern = functools.partial(_attn_kernel, tq=tq, tk=tk, G=G, QB=QB)
    return pl.pallas_call(
        kern,
        grid=(B, H // G, ns),
        in_specs=[
            pl.BlockSpec((1, G, qk, QB * tq), lambda b, h, i: (b, h, 0, i)),
            pl.BlockSpec((1, G, S, k.shape[-1]), lambda b, h, i: (b, h, 0, 0)),
            pl.BlockSpec((1, G, V_HEAD_DIM, S), lambda b, h, i: (b, h, 0, 0)),
            pl.BlockSpec((QB * tq, G * V_HEAD_DIM), lambda b, h, i: (b * ns + i, h)),
            pl.BlockSpec((2, tk, tq), lambda b, h, i: (0, 0, 0)),
        ],
        out_specs=pl.BlockSpec((QB * tq, G * V_HEAD_DIM), lambda b, h, i: (b * ns + i, h)),
        out_shape=jax.ShapeDtypeStruct((B * S, H * V_HEAD_DIM), BF16),
        scratch_shapes=[pltpu.VMEM((QB * G, 1, tq), F32), pltpu.VMEM((QB * G, 1, tq), F32),
                        pltpu.VMEM((QB * G, V_HEAD_DIM, tq), F32),
                        pltpu.VMEM((2, tk, tq), F32), pltpu.VMEM((2, tk, tq), BF16),
                        pltpu.VMEM((2, 1, tq), F32), pltpu.VMEM((4, 1, tq), F32)],
        compiler_params=_params("parallel", "parallel", "arbitrary"),
        name="attention",
    )(q, k, vt, sz, bias)


def _rope_tables(S):
    half = QK_ROPE_DIM // 2
    inv_freq = ROPE_BASE ** (-jnp.arange(half, dtype=F32) / half)
    ang = jnp.arange(S, dtype=jnp.int32).astype(F32)[:, None] * inv_freq[None, :]
    return jnp.cos(ang), jnp.sin(ang)


def kernel(x, c, w_ada, b_ada, ln_g, ln_b, a_w_in, a_w_dw, a_b_dw, a_norm_g, a_norm_b, a_w_out,
           b_w_in, b_q_norm_g, b_w_qb, b_w_out, kv_w_a, kv_norm_g, kv_w_b):
    B, S, D = x.shape
    depth = w_ada.shape[0]
    n_a = a_w_in.shape[0]
    alpha = (2.0 * depth) ** 0.25
    H = b_w_out.shape[1] // V_HEAD_DIM
    q_rank = b_q_norm_g.shape[1]
    kv_rank = kv_norm_g.shape[0]
    qk = QK_NOPE_DIM + QK_ROPE_DIM
    half = QK_ROPE_DIM // 2

    mod = _ada_mod(c, w_ada, b_ada)
    cos, sin = _rope_tables(S)
    k_tab = jnp.concatenate([cos, cos, -sin, sin], axis=1)

    x2 = x.reshape(B * S, D)
    kv = None
    for layer in range(depth):
        if layer == n_a:
            rope_w = kv_w_a[:, kv_rank:]
            w_kva_ext = jnp.concatenate(
                [kv_w_a, rope_w[:, half:], rope_w[:, :half]], axis=1).astype(BF16)
            w_kvb = kv_w_b.reshape(kv_rank, H, QK_NOPE_DIM + V_HEAD_DIM)
            w_k = w_kvb[:, :, :QK_NOPE_DIM].reshape(kv_rank, H * QK_NOPE_DIM).astype(BF16)
            w_vt = w_kvb[:, :, QK_NOPE_DIM:].reshape(kv_rank, H * V_HEAD_DIM).T.astype(BF16)
            kv = _kv_proj(x2, w_kva_ext, kv_norm_g, k_tab, w_k, w_vt, B, S, H)
        h = _ln_modulate(x2, mod[layer], S)
        if layer < n_a:
            i = layer
            v, sz = _glu_proj(h, a_w_in[i].astype(BF16))
            act = _conv_norm(v, sz, a_w_dw[i], a_b_dw[i], a_norm_g[i], a_norm_b[i], B, S)
            w_out = a_w_out[i]
        else:
            j = layer - n_a
            w_in = b_w_in[j]
            q = _q_proj(h, w_in[:, :q_rank].astype(BF16), b_q_norm_g[j], cos.T, sin.T,
                        b_w_qb[j].T.astype(BF16), B, S, H, qk ** -0.5 * math.log2(math.e))
            sz = _silu_proj(h, w_in[:, q_rank:].astype(BF16))
            act = _attention(q, kv[0], kv[1], sz, B, S, H)
            w_out = b_w_out[j]
        x2 = _out_norm(act, w_out.astype(BF16), x2, mod[layer], ln_g[layer], ln_b[layer], alpha, S)
    return x2.reshape(B, S, D)
```

```python
import functools
import math

import jax
import jax.numpy as jnp
from jax import lax
from jax.experimental import pallas as pl
from jax.experimental.pallas import tpu as pltpu

QK_NOPE_DIM = 128
QK_ROPE_DIM = 64
V_HEAD_DIM = 128
ROPE_BASE = 10000.0
LN_EPS = 1e-5
RMS_EPS = 1e-6

V7X_LANES = 128
V7X_VMEM_BYTES = 64 * 1024 * 1024
VMEM_LIMIT_BYTES = 56 * 1024 * 1024

F32 = jnp.float32
BF16 = jnp.bfloat16


def _params(*semantics):
    return pltpu.CompilerParams(dimension_semantics=semantics,
                                vmem_limit_bytes=VMEM_LIMIT_BYTES)


def _tile(dim, want):
    t = min(dim, want)
    while dim % t:
        t //= 2
    return t


def _dot(a, b):
    return jnp.dot(a, b, preferred_element_type=F32)


def _silu(x):
    return x * jax.nn.sigmoid(x)


def _ada_kernel(c_ref, w_ref, b_ref, o_ref):
    sc = _silu(c_ref[...]).astype(BF16)
    o_ref[0] = _dot(sc, w_ref[0].astype(BF16)) + b_ref[0]


def _ada_mod(c, w_ada, b_ada):
    L, D, N = w_ada.shape
    B = c.shape[0]
    Bp = max(8, B)
    cp = jnp.pad(c, ((0, Bp - B), (0, 0)))
    tn = _tile(N, 512)
    out = pl.pallas_call(
        _ada_kernel,
        grid=(L, N // tn),
        in_specs=[
            pl.BlockSpec((Bp, D), lambda l, j: (0, 0)),
            pl.BlockSpec((1, D, tn), lambda l, j: (l, 0, j)),
            pl.BlockSpec((1, 1, tn), lambda l, j: (l, 0, j)),
        ],
        out_specs=pl.BlockSpec((1, Bp, tn), lambda l, j: (l, 0, j)),
        out_shape=jax.ShapeDtypeStruct((L, Bp, N), F32),
        compiler_params=_params("parallel", "parallel"),
        name="ada_mod",
    )(cp, w_ada, b_ada.reshape(L, 1, N))
    return out[:, :B].reshape(L, B, 3, D)


def _ln_mod_kernel(x_ref, mod_ref, h_ref):
    x = x_ref[...]
    mu = jnp.mean(x, axis=-1, keepdims=True)
    xc = x - mu
    var = jnp.mean(xc * xc, axis=-1, keepdims=True)
    y = xc * lax.rsqrt(var + LN_EPS)
    shift = mod_ref[0, 0:1, :]
    scale = mod_ref[0, 1:2, :]
    h_ref[...] = (y * (1.0 + scale) + shift).astype(BF16)


def _ln_modulate(x2, mod, S):
    T, D = x2.shape
    tm = _tile(S, 256)
    per_b = S // tm
    return pl.pallas_call(
        _ln_mod_kernel,
        grid=(T // tm,),
        in_specs=[
            pl.BlockSpec((tm, D), lambda i: (i, 0)),
            pl.BlockSpec((1, 3, D), lambda i: (i // per_b, 0, 0)),
        ],
        out_specs=pl.BlockSpec((tm, D), lambda i: (i, 0)),
        out_shape=jax.ShapeDtypeStruct((T, D), BF16),
        compiler_params=_params("parallel"),
        name="ln_modulate",
    )(x2, mod)


def _glu_kernel(h_ref, wa_ref, wg_ref, wz_ref, v_ref, sz_ref):
    h = h_ref[...]
    a = _dot(h, wa_ref[...])
    g = _dot(h, wg_ref[...])
    v_ref[...] = a * jax.nn.sigmoid(g)
    z = _dot(h, wz_ref[...])
    sz_ref[...] = _silu(z)


def _glu_proj(h, w_in):
    T, D = h.shape
    C = w_in.shape[1] // 3
    tm = _tile(T, 1024)
    tn = _tile(C, 256)
    nj = C // tn
    return pl.pallas_call(
        _glu_kernel,
        grid=(T // tm, nj),
        in_specs=[
            pl.BlockSpec((tm, D), lambda i, j: (i, 0)),
            pl.BlockSpec((D, tn), lambda i, j: (0, j)),
            pl.BlockSpec((D, tn), lambda i, j: (0, j + nj)),
            pl.BlockSpec((D, tn), lambda i, j: (0, j + 2 * nj)),
        ],
        out_specs=[
            pl.BlockSpec((tm, tn), lambda i, j: (i, j)),
            pl.BlockSpec((tm, tn), lambda i, j: (i, j)),
        ],
        out_shape=[jax.ShapeDtypeStruct((T, C), F32), jax.ShapeDtypeStruct((T, C), F32)],
        compiler_params=_params("parallel", "arbitrary"),
        name="glu_proj",
    )(h, w_in, w_in, w_in)


CONV_HALO = 32
CONV_ROWS = 64
NORM_ROWS = 16
SUBLANES = 8


def _conv_kernel(vh_ref, vm_ref, sz_ref, w_ref, bdw_ref, g_ref, b_ref, o_ref,
                 slab_ref, cv_ref, mu_ref, rs_ref,
                 *, taps, ts, C):
    s = pl.program_id(1)
    ns = C // V7X_LANES
    lead = CONV_HALO - (taps - 1)
    for sl in range(ns):
        p, par = divmod(sl, 2)
        ls = slice(sl * V7X_LANES, (sl + 1) * V7X_LANES)
        halo = vh_ref[0, :, ls]
        slab_ref[p, pl.ds(par, CONV_HALO, stride=2), :] = jnp.where(s > 0, halo, jnp.zeros_like(halo))
        slab_ref[p, pl.ds(2 * CONV_HALO + par, ts, stride=2), :] = vm_ref[0, :, ls]

    nacc = CONV_ROWS // SUBLANES

    def slab_body(sl, carry):
        p = sl // 2
        par = sl % 2
        wk = [jnp.broadcast_to(w_ref[sl, k:k + 1, :], (SUBLANES, V7X_LANES)) for k in range(taps)]
        bias = bdw_ref[sl]

        def row_body(rb, c2):
            r0 = rb * CONV_ROWS
            accs = [None] * nacc
            for o in range(lead, lead + taps + CONV_ROWS - SUBLANES):
                x = slab_ref[p, pl.ds(2 * (r0 + o) + par, SUBLANES, stride=2), :]
                for i in range(nacc):
                    k = o - lead - SUBLANES * i
                    if 0 <= k < taps:
                        t = x * wk[k]
                        accs[i] = t if accs[i] is None else accs[i] + t
            for i in range(nacc):
                rows = pl.ds(pl.multiple_of(r0 + SUBLANES * i, SUBLANES), SUBLANES)
                cv_ref[sl, rows, :] = accs[i] + bias
            return c2

        lax.fori_loop(0, ts // CONV_ROWS, row_body, 0)
        return carry

    lax.fori_loop(0, ns, slab_body, 0)

    def stats_body(rb, carry):
        rows = pl.ds(pl.multiple_of(rb * CONV_ROWS, CONV_ROWS), CONV_ROWS)
        tot = cv_ref[0, rows, :]
        for sl in range(1, ns):
            tot = tot + cv_ref[sl, rows, :]
        mu = jnp.broadcast_to(jnp.sum(tot, axis=-1, keepdims=True) * (1.0 / C),
                              (CONV_ROWS, V7X_LANES))
        sq = None
        for sl in range(ns):
            d = cv_ref[sl, rows, :] - mu
            sq = d * d if sq is None else sq + d * d
        var = jnp.sum(sq, axis=-1, keepdims=True) * (1.0 / C)
        mu_ref[rows, :] = mu
        rs_ref[rows, :] = jnp.broadcast_to(lax.rsqrt(var + LN_EPS), (CONV_ROWS, V7X_LANES))
        return carry

    lax.fori_loop(0, ts // CONV_ROWS, stats_body, 0)

    def norm_body(rg, carry):
        rows = pl.ds(pl.multiple_of(rg * NORM_ROWS, NORM_ROWS), NORM_ROWS)
        mu = mu_ref[rows, :]
        rs = rs_ref[rows, :]
        for sl in range(ns):
            ls = slice(sl * V7X_LANES, (sl + 1) * V7X_LANES)
            yn = (cv_ref[sl, rows, :] - mu) * rs * g_ref[:, ls] + b_ref[:, ls]
            o_ref[rows, ls] = (_silu(yn) * sz_ref[rows, ls]).astype(BF16)
        return carry

    lax.fori_loop(0, ts // NORM_ROWS, norm_body, 0)


def _conv_norm(v, sz, w_dw, b_dw, g_cn, b_cn, B, S):
    T, C = v.shape
    taps = w_dw.shape[0]
    ns = C // V7X_LANES
    assert taps - 1 <= CONV_HALO and ns % 2 == 0
    ts = _tile(S, 256)
    assert ts % CONV_HALO == 0 and ts % CONV_ROWS == 0 and ts % NORM_ROWS == 0
    per_b = S // ts
    hb = ts // CONV_HALO
    v3 = v.reshape(B, S, C)
    w_slab = w_dw.reshape(taps, ns, V7X_LANES).transpose(1, 0, 2)
    b_slab = b_dw.reshape(ns, 1, V7X_LANES)
    kern = functools.partial(_conv_kernel, taps=taps, ts=ts, C=C)
    row = lambda a: a.reshape(1, C)
    return pl.pallas_call(
        kern,
        grid=(B, per_b),
        in_specs=[
            pl.BlockSpec((1, CONV_HALO, C), lambda b, s: (b, jnp.maximum(s * hb - 1, 0), 0)),
            pl.BlockSpec((1, ts, C), lambda b, s: (b, s, 0)),
            pl.BlockSpec((ts, C), lambda b, s: (b * per_b + s, 0)),
            pl.BlockSpec((ns, taps, V7X_LANES), lambda b, s: (0, 0, 0)),
            pl.BlockSpec((ns, 1, V7X_LANES), lambda b, s: (0, 0, 0)),
            pl.BlockSpec((1, C), lambda b, s: (0, 0)),
            pl.BlockSpec((1, C), lambda b, s: (0, 0)),
        ],
        out_specs=pl.BlockSpec((ts, C), lambda b, s: (b * per_b + s, 0)),
        out_shape=jax.ShapeDtypeStruct((T, C), BF16),
        scratch_shapes=[pltpu.VMEM((ns // 2, 2 * (ts + CONV_HALO), V7X_LANES), F32),
                        pltpu.VMEM((ns, ts, V7X_LANES), F32),
                        pltpu.VMEM((ts, V7X_LANES), F32), pltpu.VMEM((ts, V7X_LANES), F32)],
        compiler_params=_params("parallel", "parallel"),
        name="conv_norm",
    )(v3, v3, sz, w_slab, b_slab, row(g_cn), row(b_cn))


OUT_NORM_TK = 4096


def _out_kernel(act_ref, w_ref, x_ref, mod_ref, g_ref, b_ref, o_ref, y_ref,
                *, alpha, nk, nj, tn, D):
    kk = pl.program_id(1)
    j = pl.program_id(2)
    out = _dot(act_ref[...], w_ref[...])

    def residual(total):
        gate = 1.0 + mod_ref[0, 2:3, :]
        y_ref[j] = alpha * x_ref[...] + gate * total

    if nk == 1:
        residual(out)
    else:
        @pl.when(kk == 0)
        def _():
            y_ref[j] = out

        @pl.when(jnp.logical_and(kk > 0, kk < nk - 1))
        def _():
            y_ref[j] += out

        @pl.when(kk == nk - 1)
        def _():
            residual(y_ref[j] + out)

    @pl.when(jnp.logical_and(kk == nk - 1, j == nj - 1))
    def _():
        tot = jnp.sum(y_ref[0], axis=-1, keepdims=True)
        for t in range(1, nj):
            tot = tot + jnp.sum(y_ref[t], axis=-1, keepdims=True)
        mu = tot * (1.0 / D)
        sq = jnp.zeros_like(mu)
        for t in range(nj):
            d = y_ref[t] - mu
            sq = sq + jnp.sum(d * d, axis=-1, keepdims=True)
        rs = lax.rsqrt(sq * (1.0 / D) + LN_EPS)
        for t in range(nj):
            cs = slice(t * tn, (t + 1) * tn)
            o_ref[:, cs] = (y_ref[t] - mu) * rs * g_ref[:, cs] + b_ref[:, cs]


def _out_norm(act, w_out, x2, mod, ln_g, ln_b, alpha, S):
    T, K = act.shape
    D = w_out.shape[1]
    tm = _tile(S, 512)
    tn = _tile(D, 512)
    tk = _tile(K, OUT_NORM_TK)
    nj = D // tn
    nk = K // tk
    per_b = S // tm
    kern = functools.partial(_out_kernel, alpha=alpha, nk=nk, nj=nj, tn=tn, D=D)
    return pl.pallas_call(
        kern,
        grid=(T // tm, nk, nj),
        in_specs=[
            pl.BlockSpec((tm, tk), lambda i, k, j: (i, k)),
            pl.BlockSpec((tk, tn), lambda i, k, j: (k, j)),
            pl.BlockSpec((tm, tn), lambda i, k, j: (i, j)),
            pl.BlockSpec((1, 3, tn), lambda i, k, j: (i // per_b, 0, j)),
            pl.BlockSpec((1, D), lambda i, k, j: (0, 0)),
            pl.BlockSpec((1, D), lambda i, k, j: (0, 0)),
        ],
        out_specs=pl.BlockSpec((tm, D), lambda i, k, j: (i, 0)),
        out_shape=jax.ShapeDtypeStruct((T, D), F32),
        scratch_shapes=[pltpu.VMEM((nj, tm, tn), F32)],
        compiler_params=_params("parallel", "arbitrary", "arbitrary"),
        name="out_norm",
    )(act, w_out, x2, mod, ln_g.reshape(1, D), ln_b.reshape(1, D))


K_ROW_LANES = 2 * V7X_LANES
VT_ONES_ROWS = 16
VT_ROWS = V_HEAD_DIM + VT_ONES_ROWS


def _kv_kernel(x_ref, wa_ref, g_ref, cs_ref, wk_ref, wvt_ref, k_ref, vt_ref, ckv_ref, kr_ref,
               *, R, hb):
    j = pl.program_id(1)

    @pl.when(j == 0)
    def _():
        kva = _dot(x_ref[...].astype(BF16), wa_ref[...])
        ckv = kva[:, :R]
        ms = jnp.mean(ckv * ckv, axis=-1, keepdims=True)
        ckv_ref[...] = (ckv * lax.rsqrt(ms + RMS_EPS) * g_ref[...]).astype(BF16)
        u = kva[:, R:] * cs_ref[...]
        kr_ref[...] = (u[:, :QK_ROPE_DIM] + u[:, QK_ROPE_DIM:]).astype(BF16)

    qk = QK_NOPE_DIM + QK_ROPE_DIM
    ckv = ckv_ref[...]
    kn = _dot(ckv, wk_ref[...])
    vt = lax.dot_general(wvt_ref[...], ckv, (((1,), (1,)), ((), ())), preferred_element_type=F32)
    for h in range(hb):
        k_ref[0, h, :, 0:QK_NOPE_DIM] = kn[:, h * QK_NOPE_DIM:(h + 1) * QK_NOPE_DIM].astype(BF16)
        k_ref[0, h, :, QK_NOPE_DIM:qk] = kr_ref[...]
        k_ref[0, h, :, qk:] = jnp.zeros((k_ref.shape[2], K_ROW_LANES - qk), BF16)
        vt_ref[0, h, 0:V_HEAD_DIM, :] = vt[h * V_HEAD_DIM:(h + 1) * V_HEAD_DIM, :].astype(BF16)
        vt_ref[0, h, V_HEAD_DIM:, :] = jnp.ones((VT_ONES_ROWS, vt.shape[1]), BF16)


def _kv_proj(x2, w_kva_ext, g_kv, cs_tab, w_k, w_vt, B, S, H):
    T, D = x2.shape
    R = g_kv.shape[0]
    tm = _tile(S, 512)
    hb = _tile(H, 8)
    per_b = S // tm
    qk = QK_NOPE_DIM + QK_ROPE_DIM
    kern = functools.partial(_kv_kernel, R=R, hb=hb)
    return pl.pallas_call(
        kern,
        grid=(T // tm, H // hb),
        in_specs=[
            pl.BlockSpec((tm, D), lambda i, j: (i, 0)),
            pl.BlockSpec((D, R + 2 * QK_ROPE_DIM), lambda i, j: (0, 0)),
            pl.BlockSpec((1, R), lambda i, j: (0, 0)),
            pl.BlockSpec((tm, 2 * QK_ROPE_DIM), lambda i, j: (i % per_b, 0)),
            pl.BlockSpec((R, hb * QK_NOPE_DIM), lambda i, j: (0, j)),
            pl.BlockSpec((hb * V_HEAD_DIM, R), lambda i, j: (j, 0)),
        ],
        out_specs=[
            pl.BlockSpec((1, hb, tm, K_ROW_LANES), lambda i, j: (i // per_b, j, i % per_b, 0)),
            pl.BlockSpec((1, hb, VT_ROWS, tm), lambda i, j: (i // per_b, j, 0, i % per_b)),
        ],
        out_shape=[jax.ShapeDtypeStruct((B, H, S, K_ROW_LANES), BF16),
                   jax.ShapeDtypeStruct((B, H, VT_ROWS, S), BF16)],
        scratch_shapes=[pltpu.VMEM((tm, R), BF16), pltpu.VMEM((tm, QK_ROPE_DIM), BF16)],
        compiler_params=_params("parallel", "arbitrary"),
        name="kv_proj",
    )(x2, w_kva_ext, g_kv.reshape(1, R), cs_tab, w_k, w_vt)


def _q_kernel(h_ref, wc_ref, g_ref, cos_ref, sin_ref, wqt_ref, q_ref, cq_ref, *, hb, scale):
    j = pl.program_id(1)

    @pl.when(j == 0)
    def _():
        cq = _dot(h_ref[...], wc_ref[...])
        ms = jnp.mean(cq * cq, axis=-1, keepdims=True)
        cq_ref[...] = (cq * lax.rsqrt(ms + RMS_EPS) * g_ref[...]).astype(BF16)

    qt = lax.dot_general(wqt_ref[...], cq_ref[...], (((1,), (1,)), ((), ())),
                         preferred_element_type=F32) * scale
    qk = QK_NOPE_DIM + QK_ROPE_DIM
    half = QK_ROPE_DIM // 2
    cos = cos_ref[...]
    sin = sin_ref[...]
    for h in range(hb):
        r0 = h * qk
        x1 = qt[r0 + QK_NOPE_DIM:r0 + QK_NOPE_DIM + half, :]
        x2 = qt[r0 + QK_NOPE_DIM + half:r0 + qk, :]
        q_ref[0, h, 0:QK_NOPE_DIM, :] = qt[r0:r0 + QK_NOPE_DIM, :].astype(BF16)
        q_ref[0, h, QK_NOPE_DIM:QK_NOPE_DIM + half, :] = (x1 * cos - x2 * sin).astype(BF16)
        q_ref[0, h, QK_NOPE_DIM + half:qk, :] = (x1 * sin + x2 * cos).astype(BF16)


def _q_proj(h, w_cq, g_q, cos_t, sin_t, w_qt, B, S, H, scale):
    T, D = h.shape
    R = g_q.shape[0]
    tm = _tile(S, 512)
    hb = _tile(H, 8)
    per_b = S // tm
    qk = QK_NOPE_DIM + QK_ROPE_DIM
    half = QK_ROPE_DIM // 2
    kern = functools.partial(_q_kernel, hb=hb, scale=scale)
    return pl.pallas_call(
        kern,
        grid=(T // tm, H // hb),
        in_specs=[
            pl.BlockSpec((tm, D), lambda i, j: (i, 0)),
            pl.BlockSpec((D, R), lambda i, j: (0, 0)),
            pl.BlockSpec((1, R), lambda i, j: (0, 0)),
            pl.BlockSpec((half, tm), lambda i, j: (0, i % per_b)),
            pl.BlockSpec((half, tm), lambda i, j: (0, i % per_b)),
            pl.BlockSpec((hb * qk, R), lambda i, j: (j, 0)),
        ],
        out_specs=pl.BlockSpec((1, hb, qk, tm), lambda i, j: (i // per_b, j, 0, i % per_b)),
        out_shape=jax.ShapeDtypeStruct((B, H, qk, S), BF16),
        scratch_shapes=[pltpu.VMEM((tm, R), BF16)],
        compiler_params=_params("parallel", "arbitrary"),
        name="q_proj",
    )(h, w_cq, g_q.reshape(1, R), cos_t, sin_t, w_qt)


def _silu_proj_kernel(h_ref, w_ref, o_ref):
    o_ref[...] = _silu(_dot(h_ref[...], w_ref[...]))


def _silu_proj(h, w_z):
    T, D = h.shape
    N = w_z.shape[1]
    tm = _tile(T, 1024)
    tn = _tile(N, 512)
    return pl.pallas_call(
        _silu_proj_kernel,
        grid=(T // tm, N // tn),
        in_specs=[
            pl.BlockSpec((tm, D), lambda i, j: (i, 0)),
            pl.BlockSpec((D, tn), lambda i, j: (0, j)),
        ],
        out_specs=pl.BlockSpec((tm, tn), lambda i, j: (i, j)),
        out_shape=jax.ShapeDtypeStruct((T, N), F32),
        compiler_params=_params("parallel", "arbitrary"),
        name="silu_proj",
    )(h, w_z)


ATTN_HEADS_PER_STEP = 4
ATTN_QBLOCKS_PER_STEP = 4
ATTN_EXP_ROWS = 16


def _attn_kernel(q_ref, k_ref, vt_ref, sz_ref, bias_ref, o_ref,
                 m_ref, acc_ref, s_ref, p_ref, mb_ref, al_ref, *, tq, tk, G, QB):
    q0 = pl.program_id(2) * QB
    m_ref[...] = jnp.full(m_ref.shape, -jnp.inf, F32)
    acc_ref[...] = jnp.zeros(acc_ref.shape, F32)
    lg = G.bit_length() - 1
    ch = ATTN_EXP_ROWS

    def first_item(qb):
        return G * (qb * (q0 + 1) + ((qb * (qb - 1)) >> 1))

    n = first_item(QB)

    def item(it):
        qb = 0
        for u in range(1, QB):
            qb = qb + (it >= first_item(u)).astype(jnp.int32)
        local = it - first_item(qb)
        g = jnp.bitwise_and(local, G - 1)
        ki = jnp.right_shift(local, lg)
        return qb, g, ki, pl.multiple_of(ki * tk, tk)

    def scores(it, slot):
        qb, g, ki, k0 = item(it)
        st = qb * G + g
        q = q_ref[0, g, :, pl.ds(pl.multiple_of(qb * tq, tq), tq)]
        s = _dot(k_ref[0, g, pl.ds(k0, tk), 0:q.shape[0]], q)
        s = s + bias_ref[(ki == q0 + qb).astype(jnp.int32)]
        s_ref[slot] = s
        mx = jnp.max(s.reshape(tk // SUBLANES, SUBLANES, tq), axis=0)
        m_prev = m_ref[st]
        m_new = jnp.maximum(m_prev, jnp.max(mx, axis=0, keepdims=True))
        m_ref[st] = m_new
        mb_ref[slot] = m_new
        al_ref[jnp.bitwise_and(it, 3)] = jnp.exp2(m_prev - m_new)

    def softmax(it, slot):
        mb = jnp.broadcast_to(mb_ref[slot], (ch, tq))
        for c in range(tk // ch):
            p = jnp.exp2(s_ref[slot, c * ch:(c + 1) * ch, :] - mb)
            p_ref[slot, c * ch:(c + 1) * ch, :] = p.astype(BF16)

    def accumulate(it, slot):
        qb, g, _, k0 = item(it)
        st = qb * G + g
        pv = _dot(vt_ref[0, g, :, pl.ds(k0, tk)], p_ref[slot])
        acc_ref[st] = al_ref[jnp.bitwise_and(it, 3)] * acc_ref[st] + pv

    scores(0, 0)
    scores(1, 1)
    softmax(0, 0)

    def steady(t, slot):
        accumulate(t - 2, slot)
        softmax(t - 1, 1 - slot)
        scores(t, slot)

    def body(j, carry):
        steady(2 * j, 0)
        steady(2 * j + 1, 1)
        return carry

    lax.fori_loop(1, n // 2, body, 0)
    accumulate(n - 2, 0)
    softmax(n - 1, 1)
    accumulate(n - 1, 1)

    for qb in range(QB):
        rows = slice(qb * tq, (qb + 1) * tq)
        for g in range(G):
            cs = slice(g * V_HEAD_DIM, (g + 1) * V_HEAD_DIM)
            acc = acc_ref[qb * G + g]
            o = (acc[0:V_HEAD_DIM] / acc[V_HEAD_DIM:V_HEAD_DIM + 1]).T
            o_ref[rows, cs] = (o * sz_ref[rows, cs]).astype(BF16)


def _attention(q, k, vt, sz, B, S, H):
    tq = tk = _tile(S, 512)
    G = _tile(H, ATTN_HEADS_PER_STEP)
    assert G >= 2 and G & (G - 1) == 0
    nq = S // tq
    QB = _tile(nq, ATTN_QBLOCKS_PER_STEP)
    ns = nq // QB
    qk = q.shape[2]
    r = lax.broadcasted_iota(jnp.int32, (tk, tq), 0)
    c = lax.broadcasted_iota(jnp.int32, (tk, tq), 1)
    bias = jnp.stack([jnp.zeros((tk, tq), F32), jnp.where(c >= r, 0.0, -jnp.inf).astype(F32)])
    kern = functools.partial(_attn_kernel, tq=tq, tk=tk, G=G, QB=QB)
    return pl.pallas_call(
        kern,
        grid=(B, H // G, ns),
        in_specs=[
            pl.BlockSpec((1, G, qk, QB * tq), lambda b, h, i: (b, h, 0, i)),
            pl.BlockSpec((1, G, S, k.shape[-1]), lambda b, h, i: (b, h, 0, 0)),
            pl.BlockSpec((1, G, VT_ROWS, S), lambda b, h, i: (b, h, 0, 0)),
            pl.BlockSpec((QB * tq, G * V_HEAD_DIM), lambda b, h, i: (b * ns + i, h)),
            pl.BlockSpec((2, tk, tq), lambda b, h, i: (0, 0, 0)),
        ],
        out_specs=pl.BlockSpec((QB * tq, G * V_HEAD_DIM), lambda b, h, i: (b * ns + i, h)),
        out_shape=jax.ShapeDtypeStruct((B * S, H * V_HEAD_DIM), BF16),
        scratch_shapes=[pltpu.VMEM((QB * G, 1, tq), F32),
                        pltpu.VMEM((QB * G, VT_ROWS, tq), F32),
                        pltpu.VMEM((2, tk, tq), F32), pltpu.VMEM((2, tk, tq), BF16),
                        pltpu.VMEM((2, 1, tq), F32), pltpu.VMEM((4, 1, tq), F32)],
        compiler_params=_params("parallel", "parallel", "arbitrary"),
        name="attention",
    )(q, k, vt, sz, bias)


def _rope_tables(S):
    half = QK_ROPE_DIM // 2
    inv_freq = ROPE_BASE ** (-jnp.arange(half, dtype=F32) / half)
    ang = jnp.arange(S, dtype=jnp.int32).astype(F32)[:, None] * inv_freq[None, :]
    return jnp.cos(ang), jnp.sin(ang)


def kernel(x, c, w_ada, b_ada, ln_g, ln_b, a_w_in, a_w_dw, a_b_dw, a_norm_g, a_norm_b, a_w_out,
           b_w_in, b_q_norm_g, b_w_qb, b_w_out, kv_w_a, kv_norm_g, kv_w_b):
    B, S, D = x.shape
    depth = w_ada.shape[0]
    n_a = a_w_in.shape[0]
    alpha = (2.0 * depth) ** 0.25
    H = b_w_out.shape[1] // V_HEAD_DIM
    q_rank = b_q_norm_g.shape[1]
    kv_rank = kv_norm_g.shape[0]
    qk = QK_NOPE_DIM + QK_ROPE_DIM
    half = QK_ROPE_DIM // 2

    mod = _ada_mod(c, w_ada, b_ada)
    cos, sin = _rope_tables(S)
    k_tab = jnp.concatenate([cos, cos, -sin, sin], axis=1)

    x2 = x.reshape(B * S, D)
    kv = None
    for layer in range(depth):
        if layer == n_a:
            rope_w = kv_w_a[:, kv_rank:]
            w_kva_ext = jnp.concatenate(
                [kv_w_a, rope_w[:, half:], rope_w[:, :half]], axis=1).astype(BF16)
            w_kvb = kv_w_b.reshape(kv_rank, H, QK_NOPE_DIM + V_HEAD_DIM)
            w_k = w_kvb[:, :, :QK_NOPE_DIM].reshape(kv_rank, H * QK_NOPE_DIM).astype(BF16)
            w_vt = w_kvb[:, :, QK_NOPE_DIM:].reshape(kv_rank, H * V_HEAD_DIM).T.astype(BF16)
            kv = _kv_proj(x2, w_kva_ext, kv_norm_g, k_tab, w_k, w_vt, B, S, H)
        h = _ln_modulate(x2, mod[layer], S)
        if layer < n_a:
            i = layer
            v, sz = _glu_proj(h, a_w_in[i].astype(BF16))
            act = _conv_norm(v, sz, a_w_dw[i], a_b_dw[i], a_norm_g[i], a_norm_b[i], B, S)
            w_out = a_w_out[i]
        else:
            j = layer - n_a
            w_in = b_w_in[j]
            q = _q_proj(h, w_in[:, :q_rank].astype(BF16), b_q_norm_g[j], cos.T, sin.T,
                        b_w_qb[j].T.astype(BF16), B, S, H, qk ** -0.5 * math.log2(math.e))
            sz = _silu_proj(h, w_in[:, q_rank:].astype(BF16))
            act = _attention(q, kv[0], kv[1], sz, B, S, H)
            w_out = b_w_out[j]
        x2 = _out_norm(act, w_out.astype(BF16), x2, mod[layer], ln_g[layer], ln_b[layer], alpha, S)
    return x2.reshape(B, S, D)
```

```python
import functools
import math

import jax
import jax.numpy as jnp
from jax import lax
from jax.experimental import pallas as pl
from jax.experimental.pallas import tpu as pltpu

QK_NOPE_DIM = 128
QK_ROPE_DIM = 64
V_HEAD_DIM = 128
ROPE_BASE = 10000.0
LN_EPS = 1e-5
RMS_EPS = 1e-6

V7X_LANES = 128
V7X_VMEM_BYTES = 64 * 1024 * 1024
VMEM_LIMIT_BYTES = 60 * 1024 * 1024

F32 = jnp.float32
BF16 = jnp.bfloat16


def _params(*semantics):
    return pltpu.CompilerParams(dimension_semantics=semantics,
                                vmem_limit_bytes=VMEM_LIMIT_BYTES)


def _tile(dim, want):
    t = min(dim, want)
    while dim % t:
        t //= 2
    return t


def _dot(a, b):
    return jnp.dot(a, b, preferred_element_type=F32)


def _silu(x):
    return x * jax.nn.sigmoid(x)


def _ada_kernel(c_ref, w_ref, b_ref, o_ref):
    sc = _silu(c_ref[...]).astype(BF16)
    o_ref[0] = _dot(sc, w_ref[0].astype(BF16)) + b_ref[0]


def _ada_mod(c, w_ada, b_ada):
    L, D, N = w_ada.shape
    B = c.shape[0]
    Bp = max(8, B)
    cp = jnp.pad(c, ((0, Bp - B), (0, 0)))
    tn = _tile(N, 512)
    out = pl.pallas_call(
        _ada_kernel,
        grid=(L, N // tn),
        in_specs=[
            pl.BlockSpec((Bp, D), lambda l, j: (0, 0)),
            pl.BlockSpec((1, D, tn), lambda l, j: (l, 0, j)),
            pl.BlockSpec((1, 1, tn), lambda l, j: (l, 0, j)),
        ],
        out_specs=pl.BlockSpec((1, Bp, tn), lambda l, j: (l, 0, j)),
        out_shape=jax.ShapeDtypeStruct((L, Bp, N), F32),
        compiler_params=_params("parallel", "parallel"),
        name="ada_mod",
    )(cp, w_ada, b_ada.reshape(L, 1, N))
    return out[:, :B].reshape(L, B, 3, D)


def _ln_mod_kernel(x_ref, mod_ref, h_ref):
    x = x_ref[...]
    mu = jnp.mean(x, axis=-1, keepdims=True)
    xc = x - mu
    var = jnp.mean(xc * xc, axis=-1, keepdims=True)
    y = xc * lax.rsqrt(var + LN_EPS)
    shift = mod_ref[0, 0:1, :]
    scale = mod_ref[0, 1:2, :]
    h_ref[...] = (y * (1.0 + scale) + shift).astype(BF16)


def _ln_modulate(x2, mod, S):
    T, D = x2.shape
    tm = _tile(S, 256)
    per_b = S // tm
    return pl.pallas_call(
        _ln_mod_kernel,
        grid=(T // tm,),
        in_specs=[
            pl.BlockSpec((tm, D), lambda i: (i, 0)),
            pl.BlockSpec((1, 3, D), lambda i: (i // per_b, 0, 0)),
        ],
        out_specs=pl.BlockSpec((tm, D), lambda i: (i, 0)),
        out_shape=jax.ShapeDtypeStruct((T, D), BF16),
        compiler_params=_params("parallel"),
        name="ln_modulate",
    )(x2, mod)


def _glu_kernel(h_ref, wa_ref, wg_ref, wz_ref, v_ref, sz_ref):
    h = h_ref[...]
    a = _dot(h, wa_ref[...])
    g = _dot(h, wg_ref[...])
    v_ref[...] = a * jax.nn.sigmoid(g)
    z = _dot(h, wz_ref[...])
    sz_ref[...] = _silu(z)


def _glu_proj(h, w_in):
    T, D = h.shape
    C = w_in.shape[1] // 3
    tm = _tile(T, 1024)
    tn = _tile(C, 512)
    nj = C // tn
    return pl.pallas_call(
        _glu_kernel,
        grid=(T // tm, nj),
        in_specs=[
            pl.BlockSpec((tm, D), lambda i, j: (i, 0)),
            pl.BlockSpec((D, tn), lambda i, j: (0, j)),
            pl.BlockSpec((D, tn), lambda i, j: (0, j + nj)),
            pl.BlockSpec((D, tn), lambda i, j: (0, j + 2 * nj)),
        ],
        out_specs=[
            pl.BlockSpec((tm, tn), lambda i, j: (i, j)),
            pl.BlockSpec((tm, tn), lambda i, j: (i, j)),
        ],
        out_shape=[jax.ShapeDtypeStruct((T, C), F32), jax.ShapeDtypeStruct((T, C), F32)],
        compiler_params=_params("parallel", "arbitrary"),
        name="glu_proj",
    )(h, w_in, w_in, w_in)


CONV_HALO = 32
CONV_ROWS = 64
NORM_ROWS = 16
SUBLANES = 8


def _conv_kernel(vh_ref, vm_ref, sz_ref, w_ref, bdw_ref, g_ref, b_ref, o_ref,
                 slab_ref, cv_ref, mu_ref, rs_ref,
                 *, taps, ts, C):
    s = pl.program_id(1)
    ns = C // V7X_LANES
    lead = CONV_HALO - (taps - 1)
    for sl in range(ns):
        p, par = divmod(sl, 2)
        ls = slice(sl * V7X_LANES, (sl + 1) * V7X_LANES)
        halo = vh_ref[0, :, ls]
        slab_ref[p, pl.ds(par, CONV_HALO, stride=2), :] = jnp.where(s > 0, halo, jnp.zeros_like(halo))
        slab_ref[p, pl.ds(2 * CONV_HALO + par, ts, stride=2), :] = vm_ref[0, :, ls]

    nacc = CONV_ROWS // SUBLANES

    def slab_body(sl, carry):
        p = sl // 2
        par = sl % 2
        wk = [jnp.broadcast_to(w_ref[sl, k:k + 1, :], (SUBLANES, V7X_LANES)) for k in range(taps)]
        bias = bdw_ref[sl]

        def row_body(rb, c2):
            r0 = rb * CONV_ROWS
            accs = [None] * nacc
            for o in range(lead, lead + taps + CONV_ROWS - SUBLANES):
                x = slab_ref[p, pl.ds(2 * (r0 + o) + par, SUBLANES, stride=2), :]
                for i in range(nacc):
                    k = o - lead - SUBLANES * i
                    if 0 <= k < taps:
                        t = x * wk[k]
                        accs[i] = t if accs[i] is None else accs[i] + t
            for i in range(nacc):
                rows = pl.ds(pl.multiple_of(r0 + SUBLANES * i, SUBLANES), SUBLANES)
                cv_ref[sl, rows, :] = accs[i] + bias
            return c2

        lax.fori_loop(0, ts // CONV_ROWS, row_body, 0)
        return carry

    lax.fori_loop(0, ns, slab_body, 0)

    def stats_body(rb, carry):
        rows = pl.ds(pl.multiple_of(rb * CONV_ROWS, CONV_ROWS), CONV_ROWS)
        tot = cv_ref[0, rows, :]
        for sl in range(1, ns):
            tot = tot + cv_ref[sl, rows, :]
        mu = jnp.broadcast_to(jnp.sum(tot, axis=-1, keepdims=True) * (1.0 / C),
                              (CONV_ROWS, V7X_LANES))
        sq = None
        for sl in range(ns):
            d = cv_ref[sl, rows, :] - mu
            sq = d * d if sq is None else sq + d * d
        var = jnp.sum(sq, axis=-1, keepdims=True) * (1.0 / C)
        mu_ref[rows, :] = mu
        rs_ref[rows, :] = jnp.broadcast_to(lax.rsqrt(var + LN_EPS), (CONV_ROWS, V7X_LANES))
        return carry

    lax.fori_loop(0, ts // CONV_ROWS, stats_body, 0)

    def norm_body(rg, carry):
        rows = pl.ds(pl.multiple_of(rg * NORM_ROWS, NORM_ROWS), NORM_ROWS)
        mu = mu_ref[rows, :]
        rs = rs_ref[rows, :]
        for sl in range(ns):
            ls = slice(sl * V7X_LANES, (sl + 1) * V7X_LANES)
            yn = (cv_ref[sl, rows, :] - mu) * rs * g_ref[:, ls] + b_ref[:, ls]
            o_ref[rows, ls] = (_silu(yn) * sz_ref[rows, ls]).astype(BF16)
        return carry

    lax.fori_loop(0, ts // NORM_ROWS, norm_body, 0)


def _conv_norm(v, sz, w_dw, b_dw, g_cn, b_cn, B, S):
    T, C = v.shape
    taps = w_dw.shape[0]
    ns = C // V7X_LANES
    assert taps - 1 <= CONV_HALO and ns % 2 == 0
    ts = _tile(S, 256)
    assert ts % CONV_HALO == 0 and ts % CONV_ROWS == 0 and ts % NORM_ROWS == 0
    per_b = S // ts
    hb = ts // CONV_HALO
    v3 = v.reshape(B, S, C)
    w_slab = w_dw.reshape(taps, ns, V7X_LANES).transpose(1, 0, 2)
    b_slab = b_dw.reshape(ns, 1, V7X_LANES)
    kern = functools.partial(_conv_kernel, taps=taps, ts=ts, C=C)
    row = lambda a: a.reshape(1, C)
    return pl.pallas_call(
        kern,
        grid=(B, per_b),
        in_specs=[
            pl.BlockSpec((1, CONV_HALO, C), lambda b, s: (b, jnp.maximum(s * hb - 1, 0), 0)),
            pl.BlockSpec((1, ts, C), lambda b, s: (b, s, 0)),
            pl.BlockSpec((ts, C), lambda b, s: (b * per_b + s, 0)),
            pl.BlockSpec((ns, taps, V7X_LANES), lambda b, s: (0, 0, 0)),
            pl.BlockSpec((ns, 1, V7X_LANES), lambda b, s: (0, 0, 0)),
            pl.BlockSpec((1, C), lambda b, s: (0, 0)),
            pl.BlockSpec((1, C), lambda b, s: (0, 0)),
        ],
        out_specs=pl.BlockSpec((ts, C), lambda b, s: (b * per_b + s, 0)),
        out_shape=jax.ShapeDtypeStruct((T, C), BF16),
        scratch_shapes=[pltpu.VMEM((ns // 2, 2 * (ts + CONV_HALO), V7X_LANES), F32),
                        pltpu.VMEM((ns, ts, V7X_LANES), F32),
                        pltpu.VMEM((ts, V7X_LANES), F32), pltpu.VMEM((ts, V7X_LANES), F32)],
        compiler_params=_params("parallel", "parallel"),
        name="conv_norm",
    )(v3, v3, sz, w_slab, b_slab, row(g_cn), row(b_cn))


OUT_NORM_TK = 4096


def _out_kernel(act_ref, w_ref, x_ref, mod_ref, g_ref, b_ref, *rest, alpha, nk, nj, tn, D, with_next):
    if with_next:
        modn_ref, o_ref, h_ref, y_ref = rest
    else:
        o_ref, y_ref = rest
    kk = pl.program_id(1)
    j = pl.program_id(2)
    out = _dot(act_ref[...], w_ref[...])

    def residual(total):
        gate = 1.0 + mod_ref[0, 2:3, :]
        y_ref[j] = alpha * x_ref[...] + gate * total

    if nk == 1:
        residual(out)
    else:
        @pl.when(kk == 0)
        def _():
            y_ref[j] = out

        @pl.when(jnp.logical_and(kk > 0, kk < nk - 1))
        def _():
            y_ref[j] += out

        @pl.when(kk == nk - 1)
        def _():
            residual(y_ref[j] + out)

    @pl.when(jnp.logical_and(kk == nk - 1, j == nj - 1))
    def _():
        tot = jnp.sum(y_ref[0], axis=-1, keepdims=True)
        for t in range(1, nj):
            tot = tot + jnp.sum(y_ref[t], axis=-1, keepdims=True)
        mu = tot * (1.0 / D)
        sq = jnp.zeros_like(mu)
        for t in range(nj):
            d = y_ref[t] - mu
            sq = sq + jnp.sum(d * d, axis=-1, keepdims=True)
        rs = lax.rsqrt(sq * (1.0 / D) + LN_EPS)
        for t in range(nj):
            cs = slice(t * tn, (t + 1) * tn)
            o_ref[:, cs] = (y_ref[t] - mu) * rs * g_ref[:, cs] + b_ref[:, cs]
        if with_next:
            tot = jnp.sum(o_ref[:, 0:tn], axis=-1, keepdims=True)
            for t in range(1, nj):
                tot = tot + jnp.sum(o_ref[:, t * tn:(t + 1) * tn], axis=-1, keepdims=True)
            mu = tot * (1.0 / D)
            sq = jnp.zeros_like(mu)
            for t in range(nj):
                d = o_ref[:, t * tn:(t + 1) * tn] - mu
                sq = sq + jnp.sum(d * d, axis=-1, keepdims=True)
            rs = lax.rsqrt(sq * (1.0 / D) + LN_EPS)
            for t in range(nj):
                cs = slice(t * tn, (t + 1) * tn)
                hn = (o_ref[:, cs] - mu) * rs
                h_ref[:, cs] = (hn * (1.0 + modn_ref[0, 1:2, cs]) + modn_ref[0, 0:1, cs]).astype(BF16)


def _out_norm(act, w_out, x2, mod, ln_g, ln_b, alpha, S, mod_next=None):
    T, K = act.shape
    D = w_out.shape[1]
    tm = _tile(S, 512)
    tn = _tile(D, 512)
    tk = _tile(K, OUT_NORM_TK)
    nj = D // tn
    nk = K // tk
    per_b = S // tm
    with_next = mod_next is not None
    kern = functools.partial(_out_kernel, alpha=alpha, nk=nk, nj=nj, tn=tn, D=D,
                             with_next=with_next)
    row = pl.BlockSpec((tm, D), lambda i, k, j: (i, 0))
    in_specs = [
        pl.BlockSpec((tm, tk), lambda i, k, j: (i, k)),
        pl.BlockSpec((tk, tn), lambda i, k, j: (k, j)),
        pl.BlockSpec((tm, tn), lambda i, k, j: (i, j)),
        pl.BlockSpec((1, 3, tn), lambda i, k, j: (i // per_b, 0, j)),
        pl.BlockSpec((1, D), lambda i, k, j: (0, 0)),
        pl.BlockSpec((1, D), lambda i, k, j: (0, 0)),
    ]
    args = [act, w_out, x2, mod, ln_g.reshape(1, D), ln_b.reshape(1, D)]
    out_specs, out_shape = row, jax.ShapeDtypeStruct((T, D), F32)
    if with_next:
        in_specs.append(pl.BlockSpec((1, 3, D), lambda i, k, j: (i // per_b, 0, 0)))
        args.append(mod_next)
        out_specs = [row, row]
        out_shape = [out_shape, jax.ShapeDtypeStruct((T, D), BF16)]
    return pl.pallas_call(
        kern,
        grid=(T // tm, nk, nj),
        in_specs=in_specs,
        out_specs=out_specs,
        out_shape=out_shape,
        scratch_shapes=[pltpu.VMEM((nj, tm, tn), F32)],
        compiler_params=_params("parallel", "arbitrary", "arbitrary"),
        name="out_norm",
    )(*args)


K_ROW_LANES = 2 * V7X_LANES


def _kv_kernel(x_ref, wa_ref, g_ref, cs_ref, wk_ref, wvt_ref, k_ref, vt_ref, ckv_ref, kr_ref,
               *, R, hb):
    j = pl.program_id(1)

    @pl.when(j == 0)
    def _():
        kva = _dot(x_ref[...].astype(BF16), wa_ref[...])
        ckv = kva[:, :R]
        ms = jnp.mean(ckv * ckv, axis=-1, keepdims=True)
        ckv_ref[...] = (ckv * lax.rsqrt(ms + RMS_EPS) * g_ref[...]).astype(BF16)
        u = kva[:, R:] * cs_ref[...]
        kr_ref[...] = (u[:, :QK_ROPE_DIM] + u[:, QK_ROPE_DIM:]).astype(BF16)

    qk = QK_NOPE_DIM + QK_ROPE_DIM
    ckv = ckv_ref[...]
    kn = _dot(ckv, wk_ref[...])
    vt = lax.dot_general(wvt_ref[...], ckv, (((1,), (1,)), ((), ())), preferred_element_type=F32)
    for h in range(hb):
        k_ref[0, h, :, 0:QK_NOPE_DIM] = kn[:, h * QK_NOPE_DIM:(h + 1) * QK_NOPE_DIM].astype(BF16)
        k_ref[0, h, :, QK_NOPE_DIM:qk] = kr_ref[...]
        k_ref[0, h, :, qk:] = jnp.zeros((k_ref.shape[2], K_ROW_LANES - qk), BF16)
        vt_ref[0, h] = vt[h * V_HEAD_DIM:(h + 1) * V_HEAD_DIM, :].astype(BF16)


def _kv_proj(x2, w_kva_ext, g_kv, cs_tab, w_k, w_vt, B, S, H):
    T, D = x2.shape
    R = g_kv.shape[0]
    tm = _tile(S, 512)
    hb = _tile(H, 8)
    per_b = S // tm
    qk = QK_NOPE_DIM + QK_ROPE_DIM
    kern = functools.partial(_kv_kernel, R=R, hb=hb)
    return pl.pallas_call(
        kern,
        grid=(T // tm, H // hb),
        in_specs=[
            pl.BlockSpec((tm, D), lambda i, j: (i, 0)),
            pl.BlockSpec((D, R + 2 * QK_ROPE_DIM), lambda i, j: (0, 0)),
            pl.BlockSpec((1, R), lambda i, j: (0, 0)),
            pl.BlockSpec((tm, 2 * QK_ROPE_DIM), lambda i, j: (i % per_b, 0)),
            pl.BlockSpec((R, hb * QK_NOPE_DIM), lambda i, j: (0, j)),
            pl.BlockSpec((hb * V_HEAD_DIM, R), lambda i, j: (j, 0)),
        ],
        out_specs=[
            pl.BlockSpec((1, hb, tm, K_ROW_LANES), lambda i, j: (i // per_b, j, i % per_b, 0)),
            pl.BlockSpec((1, hb, V_HEAD_DIM, tm), lambda i, j: (i // per_b, j, 0, i % per_b)),
        ],
        out_shape=[jax.ShapeDtypeStruct((B, H, S, K_ROW_LANES), BF16),
                   jax.ShapeDtypeStruct((B, H, V_HEAD_DIM, S), BF16)],
        scratch_shapes=[pltpu.VMEM((tm, R), BF16), pltpu.VMEM((tm, QK_ROPE_DIM), BF16)],
        compiler_params=_params("parallel", "arbitrary"),
        name="kv_proj",
    )(x2, w_kva_ext, g_kv.reshape(1, R), cs_tab, w_k, w_vt)


def _q_kernel(h_ref, wc_ref, g_ref, cos_ref, sin_ref, wqt_ref, q_ref, cq_ref, *, hb, scale):
    j = pl.program_id(1)

    @pl.when(j == 0)
    def _():
        cq = _dot(h_ref[...], wc_ref[...])
        ms = jnp.mean(cq * cq, axis=-1, keepdims=True)
        cq_ref[...] = (cq * lax.rsqrt(ms + RMS_EPS) * g_ref[...]).astype(BF16)

    qt = lax.dot_general(wqt_ref[...], cq_ref[...], (((1,), (1,)), ((), ())),
                         preferred_element_type=F32) * scale
    qk = QK_NOPE_DIM + QK_ROPE_DIM
    half = QK_ROPE_DIM // 2
    cos = cos_ref[...]
    sin = sin_ref[...]
    for h in range(hb):
        r0 = h * qk
        x1 = qt[r0 + QK_NOPE_DIM:r0 + QK_NOPE_DIM + half, :]
        x2 = qt[r0 + QK_NOPE_DIM + half:r0 + qk, :]
        q_ref[0, h, 0:QK_NOPE_DIM, :] = qt[r0:r0 + QK_NOPE_DIM, :].astype(BF16)
        q_ref[0, h, QK_NOPE_DIM:QK_NOPE_DIM + half, :] = (x1 * cos - x2 * sin).astype(BF16)
        q_ref[0, h, QK_NOPE_DIM + half:qk, :] = (x1 * sin + x2 * cos).astype(BF16)


def _q_proj(h, w_cq, g_q, cos_t, sin_t, w_qt, B, S, H, scale):
    T, D = h.shape
    R = g_q.shape[0]
    tm = _tile(S, 512)
    hb = _tile(H, 8)
    per_b = S // tm
    qk = QK_NOPE_DIM + QK_ROPE_DIM
    half = QK_ROPE_DIM // 2
    kern = functools.partial(_q_kernel, hb=hb, scale=scale)
    return pl.pallas_call(
        kern,
        grid=(T // tm, H // hb),
        in_specs=[
            pl.BlockSpec((tm, D), lambda i, j: (i, 0)),
            pl.BlockSpec((D, R), lambda i, j: (0, 0)),
            pl.BlockSpec((1, R), lambda i, j: (0, 0)),
            pl.BlockSpec((half, tm), lambda i, j: (0, i % per_b)),
            pl.BlockSpec((half, tm), lambda i, j: (0, i % per_b)),
            pl.BlockSpec((hb * qk, R), lambda i, j: (j, 0)),
        ],
        out_specs=pl.BlockSpec((1, hb, qk, tm), lambda i, j: (i // per_b, j, 0, i % per_b)),
        out_shape=jax.ShapeDtypeStruct((B, H, qk, S), BF16),
        scratch_shapes=[pltpu.VMEM((tm, R), BF16)],
        compiler_params=_params("parallel", "arbitrary"),
        name="q_proj",
    )(h, w_cq, g_q.reshape(1, R), cos_t, sin_t, w_qt)


def _silu_proj_kernel(h_ref, w_ref, o_ref):
    o_ref[...] = _silu(_dot(h_ref[...], w_ref[...]))


def _silu_proj(h, w_in, col0):
    T, D = h.shape
    N = w_in.shape[1] - col0
    tm = _tile(T, 1024)
    tn = _tile(math.gcd(N, col0), 512)
    c0 = col0 // tn
    return pl.pallas_call(
        _silu_proj_kernel,
        grid=(T // tm, N // tn),
        in_specs=[
            pl.BlockSpec((tm, D), lambda i, j: (i, 0)),
            pl.BlockSpec((D, tn), lambda i, j: (0, j + c0)),
        ],
        out_specs=pl.BlockSpec((tm, tn), lambda i, j: (i, j)),
        out_shape=jax.ShapeDtypeStruct((T, N), F32),
        compiler_params=_params("parallel", "arbitrary"),
        name="silu_proj",
    )(h, w_in)


ATTN_HEADS_PER_STEP = 4
ATTN_QBLOCKS_PER_STEP = 4
ATTN_EXP_ROWS = 16


def _attn_kernel(q_ref, k_ref, vt_ref, sz_ref, bias_ref, o_ref,
                 m_ref, l_ref, acc_ref, s_ref, p_ref, mb_ref, al_ref, *, tq, tk, G, QB):
    q0 = pl.program_id(2) * QB
    m_ref[...] = jnp.full(m_ref.shape, -jnp.inf, F32)
    l_ref[...] = jnp.zeros(l_ref.shape, F32)
    acc_ref[...] = jnp.zeros(acc_ref.shape, F32)
    lg = G.bit_length() - 1
    ch = ATTN_EXP_ROWS

    def first_item(qb):
        return G * (qb * (q0 + 1) + ((qb * (qb - 1)) >> 1))

    n = first_item(QB)

    def item(it):
        qb = 0
        for u in range(1, QB):
            qb = qb + (it >= first_item(u)).astype(jnp.int32)
        local = it - first_item(qb)
        g = jnp.bitwise_and(local, G - 1)
        ki = jnp.right_shift(local, lg)
        return qb, g, ki, pl.multiple_of(ki * tk, tk)

    def scores(it, slot):
        qb, g, ki, k0 = item(it)
        st = qb * G + g
        q = q_ref[0, g, :, pl.ds(pl.multiple_of(qb * tq, tq), tq)]
        s = _dot(k_ref[0, g, pl.ds(k0, tk), 0:q.shape[0]], q)
        s = s + bias_ref[(ki == q0 + qb).astype(jnp.int32)]
        s_ref[slot] = s
        mx = jnp.max(s.reshape(tk // SUBLANES, SUBLANES, tq), axis=0)
        m_prev = m_ref[st]
        m_new = jnp.maximum(m_prev, jnp.max(mx, axis=0, keepdims=True))
        m_ref[st] = m_new
        mb_ref[slot] = m_new
        al_ref[jnp.bitwise_and(it, 3)] = jnp.exp2(m_prev - m_new)

    def softmax(it, slot):
        qb, g, _, _ = item(it)
        st = qb * G + g
        mb = jnp.broadcast_to(mb_ref[slot], (ch, tq))
        ls = None
        for c in range(tk // ch):
            p = jnp.exp2(s_ref[slot, c * ch:(c + 1) * ch, :] - mb)
            ls = p if ls is None else ls + p
            p_ref[slot, c * ch:(c + 1) * ch, :] = p.astype(BF16)
        alpha = al_ref[jnp.bitwise_and(it, 3)]
        l_ref[st] = alpha * l_ref[st] + jnp.sum(ls, axis=0, keepdims=True)

    def accumulate(it, slot):
        qb, g, _, k0 = item(it)
        st = qb * G + g
        pv = _dot(vt_ref[0, g, :, pl.ds(k0, tk)], p_ref[slot])
        acc_ref[st] = al_ref[jnp.bitwise_and(it, 3)] * acc_ref[st] + pv

    scores(0, 0)
    scores(1, 1)
    softmax(0, 0)

    def steady(t, slot):
        accumulate(t - 2, slot)
        softmax(t - 1, 1 - slot)
        scores(t, slot)

    def body(j, carry):
        steady(2 * j, 0)
        steady(2 * j + 1, 1)
        return carry

    lax.fori_loop(1, n // 2, body, 0)
    accumulate(n - 2, 0)
    softmax(n - 1, 1)
    accumulate(n - 1, 1)

    for qb in range(QB):
        rows = slice(qb * tq, (qb + 1) * tq)
        for g in range(G):
            cs = slice(g * V_HEAD_DIM, (g + 1) * V_HEAD_DIM)
            o = (acc_ref[qb * G + g] / l_ref[qb * G + g]).T
            o_ref[rows, cs] = (o * sz_ref[rows, cs]).astype(BF16)


def _attention(q, k, vt, sz, B, S, H):
    tq = tk = _tile(S, 512)
    G = _tile(H, ATTN_HEADS_PER_STEP)
    assert G >= 2 and G & (G - 1) == 0
    nq = S // tq
    QB = _tile(nq, ATTN_QBLOCKS_PER_STEP)
    ns = nq // QB
    qk = q.shape[2]
    r = lax.broadcasted_iota(jnp.int32, (tk, tq), 0)
    c = lax.broadcasted_iota(jnp.int32, (tk, tq), 1)
    bias = jnp.stack([jnp.zeros((tk, tq), F32), jnp.where(c >= r, 0.0, -jnp.inf).astype(F32)])
    kern = functools.partial(_attn_kernel, tq=tq, tk=tk, G=G, QB=QB)
    return pl.pallas_call(
        kern,
        grid=(B, H // G, ns),
        in_specs=[
            pl.BlockSpec((1, G, qk, QB * tq), lambda b, h, i: (b, h, 0, i)),
            pl.BlockSpec((1, G, S, k.shape[-1]), lambda b, h, i: (b, h, 0, 0)),
            pl.BlockSpec((1, G, V_HEAD_DIM, S), lambda b, h, i: (b, h, 0, 0)),
            pl.BlockSpec((QB * tq, G * V_HEAD_DIM), lambda b, h, i: (b * ns + i, h)),
            pl.BlockSpec((2, tk, tq), lambda b, h, i: (0, 0, 0)),
        ],
        out_specs=pl.BlockSpec((QB * tq, G * V_HEAD_DIM), lambda b, h, i: (b * ns + i, h)),
        out_shape=jax.ShapeDtypeStruct((B * S, H * V_HEAD_DIM), BF16),
        scratch_shapes=[pltpu.VMEM((QB * G, 1, tq), F32), pltpu.VMEM((QB * G, 1, tq), F32),
                        pltpu.VMEM((QB * G, V_HEAD_DIM, tq), F32),
                        pltpu.VMEM((2, tk, tq), F32), pltpu.VMEM((2, tk, tq), BF16),
                        pltpu.VMEM((2, 1, tq), F32), pltpu.VMEM((4, 1, tq), F32)],
        compiler_params=_params("parallel", "parallel", "arbitrary"),
        name="attention",
    )(q, k, vt, sz, bias)


def _rope_tables(S):
    half = QK_ROPE_DIM // 2
    inv_freq = ROPE_BASE ** (-jnp.arange(half, dtype=F32) / half)
    ang = jnp.arange(S, dtype=jnp.int32).astype(F32)[:, None] * inv_freq[None, :]
    return jnp.cos(ang), jnp.sin(ang)


def kernel(x, c, w_ada, b_ada, ln_g, ln_b, a_w_in, a_w_dw, a_b_dw, a_norm_g, a_norm_b, a_w_out,
           b_w_in, b_q_norm_g, b_w_qb, b_w_out, kv_w_a, kv_norm_g, kv_w_b):
    B, S, D = x.shape
    depth = w_ada.shape[0]
    n_a = a_w_in.shape[0]
    alpha = (2.0 * depth) ** 0.25
    H = b_w_out.shape[1] // V_HEAD_DIM
    q_rank = b_q_norm_g.shape[1]
    kv_rank = kv_norm_g.shape[0]
    qk = QK_NOPE_DIM + QK_ROPE_DIM
    half = QK_ROPE_DIM // 2

    mod = _ada_mod(c, w_ada, b_ada)
    cos, sin = _rope_tables(S)
    k_tab = jnp.concatenate([cos, cos, -sin, sin], axis=1)

    x2 = x.reshape(B * S, D)
    kv = None
    h = None
    for layer in range(depth):
        if layer == n_a:
            rope_w = kv_w_a[:, kv_rank:]
            w_kva_ext = jnp.concatenate(
                [kv_w_a, rope_w[:, half:], rope_w[:, :half]], axis=1).astype(BF16)
            w_kvb = kv_w_b.reshape(kv_rank, H, QK_NOPE_DIM + V_HEAD_DIM)
            w_k = w_kvb[:, :, :QK_NOPE_DIM].reshape(kv_rank, H * QK_NOPE_DIM).astype(BF16)
            w_vt = w_kvb[:, :, QK_NOPE_DIM:].reshape(kv_rank, H * V_HEAD_DIM).T.astype(BF16)
            kv = _kv_proj(x2, w_kva_ext, kv_norm_g, k_tab, w_k, w_vt, B, S, H)
        if h is None:
            h = _ln_modulate(x2, mod[layer], S)
        if layer < n_a:
            i = layer
            v, sz = _glu_proj(h, a_w_in[i].astype(BF16))
            act = _conv_norm(v, sz, a_w_dw[i], a_b_dw[i], a_norm_g[i], a_norm_b[i], B, S)
            w_out = a_w_out[i]
        else:
            j = layer - n_a
            w_in = b_w_in[j]
            w_in = w_in.astype(BF16)
            q = _q_proj(h, w_in, b_q_norm_g[j], cos.T, sin.T,
                        b_w_qb[j].T.astype(BF16), B, S, H, qk ** -0.5 * math.log2(math.e))
            sz = _silu_proj(h, w_in, q_rank)
            act = _attention(q, kv[0], kv[1], sz, B, S, H)
            w_out = b_w_out[j]
        w_out = w_out.astype(BF16)
        if layer + 1 < depth:
            x2, h = _out_norm(act, w_out, x2, mod[layer], ln_g[layer], ln_b[layer], alpha, S,
                              mod_next=mod[layer + 1])
        else:
            x2 = _out_norm(act, w_out, x2, mod[layer], ln_g[layer], ln_b[layer], alpha, S)
    return x2.reshape(B, S, D)
```

```python
import functools
import math

import jax
import jax.numpy as jnp
from jax import lax
from jax.experimental import pallas as pl
from jax.experimental.pallas import tpu as pltpu

QK_NOPE_DIM = 128
QK_ROPE_DIM = 64
V_HEAD_DIM = 128
ROPE_BASE = 10000.0
LN_EPS = 1e-5
RMS_EPS = 1e-6

V7X_LANES = 128
V7X_VMEM_BYTES = 64 * 1024 * 1024
VMEM_LIMIT_BYTES = 60 * 1024 * 1024

F32 = jnp.float32
BF16 = jnp.bfloat16


def _params(*semantics):
    return pltpu.CompilerParams(dimension_semantics=semantics,
                                vmem_limit_bytes=VMEM_LIMIT_BYTES)


def _tile(dim, want):
    t = min(dim, want)
    while dim % t:
        t //= 2
    return t


def _dot(a, b):
    return jnp.dot(a, b, preferred_element_type=F32)


def _silu(x):
    return x * jax.nn.sigmoid(x)


def _ada_kernel(c_ref, w_ref, b_ref, o_ref):
    sc = _silu(c_ref[...]).astype(BF16)
    o_ref[0] = _dot(sc, w_ref[0].astype(BF16)) + b_ref[0]


def _ada_mod(c, w_ada, b_ada):
    L, D, N = w_ada.shape
    B = c.shape[0]
    Bp = max(8, B)
    cp = jnp.pad(c, ((0, Bp - B), (0, 0)))
    tn = _tile(N, 512)
    out = pl.pallas_call(
        _ada_kernel,
        grid=(L, N // tn),
        in_specs=[
            pl.BlockSpec((Bp, D), lambda l, j: (0, 0)),
            pl.BlockSpec((1, D, tn), lambda l, j: (l, 0, j)),
            pl.BlockSpec((1, 1, tn), lambda l, j: (l, 0, j)),
        ],
        out_specs=pl.BlockSpec((1, Bp, tn), lambda l, j: (l, 0, j)),
        out_shape=jax.ShapeDtypeStruct((L, Bp, N), F32),
        compiler_params=_params("parallel", "parallel"),
        name="ada_mod",
    )(cp, w_ada, b_ada.reshape(L, 1, N))
    return out[:, :B].reshape(L, B, 3, D)


def _ln_mod_kernel(x_ref, mod_ref, h_ref):
    x = x_ref[...]
    mu = jnp.mean(x, axis=-1, keepdims=True)
    xc = x - mu
    var = jnp.mean(xc * xc, axis=-1, keepdims=True)
    y = xc * lax.rsqrt(var + LN_EPS)
    shift = mod_ref[0, 0:1, :]
    scale = mod_ref[0, 1:2, :]
    h_ref[...] = (y * (1.0 + scale) + shift).astype(BF16)


def _ln_modulate(x2, mod, S):
    T, D = x2.shape
    tm = _tile(S, 256)
    per_b = S // tm
    return pl.pallas_call(
        _ln_mod_kernel,
        grid=(T // tm,),
        in_specs=[
            pl.BlockSpec((tm, D), lambda i: (i, 0)),
            pl.BlockSpec((1, 3, D), lambda i: (i // per_b, 0, 0)),
        ],
        out_specs=pl.BlockSpec((tm, D), lambda i: (i, 0)),
        out_shape=jax.ShapeDtypeStruct((T, D), BF16),
        compiler_params=_params("parallel"),
        name="ln_modulate",
    )(x2, mod)


def _glu_kernel(h_ref, wa_ref, wg_ref, wz_ref, v_ref, sz_ref):
    h = h_ref[...]
    a = _dot(h, wa_ref[...])
    g = _dot(h, wg_ref[...])
    v_ref[...] = a * jax.nn.sigmoid(g)
    z = _dot(h, wz_ref[...])
    sz_ref[...] = _silu(z)


def _glu_proj(h, w_in):
    T, D = h.shape
    C = w_in.shape[1] // 3
    tm = _tile(T, 1024)
    tn = _tile(C, 512)
    nj = C // tn
    return pl.pallas_call(
        _glu_kernel,
        grid=(T // tm, nj),
        in_specs=[
            pl.BlockSpec((tm, D), lambda i, j: (i, 0)),
            pl.BlockSpec((D, tn), lambda i, j: (0, j)),
            pl.BlockSpec((D, tn), lambda i, j: (0, j + nj)),
            pl.BlockSpec((D, tn), lambda i, j: (0, j + 2 * nj)),
        ],
        out_specs=[
            pl.BlockSpec((tm, tn), lambda i, j: (i, j)),
            pl.BlockSpec((tm, tn), lambda i, j: (i, j)),
        ],
        out_shape=[jax.ShapeDtypeStruct((T, C), F32), jax.ShapeDtypeStruct((T, C), F32)],
        compiler_params=_params("parallel", "arbitrary"),
        name="glu_proj",
    )(h, w_in, w_in, w_in)


CONV_HALO = 32
CONV_ROWS = 64
NORM_ROWS = 16
SUBLANES = 8


def _conv_kernel(vh_ref, vm_ref, sz_ref, w_ref, bdw_ref, g_ref, b_ref, o_ref,
                 slab_ref, cv_ref, mu_ref, rs_ref,
                 *, taps, ts, C):
    s = pl.program_id(1)
    ns = C // V7X_LANES
    lead = CONV_HALO - (taps - 1)
    for sl in range(ns):
        p, par = divmod(sl, 2)
        ls = slice(sl * V7X_LANES, (sl + 1) * V7X_LANES)
        halo = vh_ref[0, :, ls]
        slab_ref[p, pl.ds(par, CONV_HALO, stride=2), :] = jnp.where(s > 0, halo, jnp.zeros_like(halo))
        slab_ref[p, pl.ds(2 * CONV_HALO + par, ts, stride=2), :] = vm_ref[0, :, ls]

    nacc = CONV_ROWS // SUBLANES

    def slab_body(sl, carry):
        p = sl // 2
        par = sl % 2
        wk = [jnp.broadcast_to(w_ref[sl, k:k + 1, :], (SUBLANES, V7X_LANES)) for k in range(taps)]
        bias = bdw_ref[sl]

        def row_body(rb, c2):
            r0 = rb * CONV_ROWS
            accs = [None] * nacc
            for o in range(lead, lead + taps + CONV_ROWS - SUBLANES):
                x = slab_ref[p, pl.ds(2 * (r0 + o) + par, SUBLANES, stride=2), :]
                for i in range(nacc):
                    k = o - lead - SUBLANES * i
                    if 0 <= k < taps:
                        t = x * wk[k]
                        accs[i] = t if accs[i] is None else accs[i] + t
            for i in range(nacc):
                rows = pl.ds(pl.multiple_of(r0 + SUBLANES * i, SUBLANES), SUBLANES)
                cv_ref[sl, rows, :] = accs[i] + bias
            return c2

        lax.fori_loop(0, ts // CONV_ROWS, row_body, 0)
        return carry

    lax.fori_loop(0, ns, slab_body, 0)

    def stats_body(rb, carry):
        rows = pl.ds(pl.multiple_of(rb * CONV_ROWS, CONV_ROWS), CONV_ROWS)
        tot = cv_ref[0, rows, :]
        for sl in range(1, ns):
            tot = tot + cv_ref[sl, rows, :]
        mu = jnp.broadcast_to(jnp.sum(tot, axis=-1, keepdims=True) * (1.0 / C),
                              (CONV_ROWS, V7X_LANES))
        sq = None
        for sl in range(ns):
            d = cv_ref[sl, rows, :] - mu
            sq = d * d if sq is None else sq + d * d
        var = jnp.sum(sq, axis=-1, keepdims=True) * (1.0 / C)
        mu_ref[rows, :] = mu
        rs_ref[rows, :] = jnp.broadcast_to(lax.rsqrt(var + LN_EPS), (CONV_ROWS, V7X_LANES))
        return carry

    lax.fori_loop(0, ts // CONV_ROWS, stats_body, 0)

    def norm_body(rg, carry):
        rows = pl.ds(pl.multiple_of(rg * NORM_ROWS, NORM_ROWS), NORM_ROWS)
        mu = mu_ref[rows, :]
        rs = rs_ref[rows, :]
        for sl in range(ns):
            ls = slice(sl * V7X_LANES, (sl + 1) * V7X_LANES)
            yn = (cv_ref[sl, rows, :] - mu) * rs * g_ref[:, ls] + b_ref[:, ls]
            o_ref[rows, ls] = (_silu(yn) * sz_ref[rows, ls]).astype(BF16)
        return carry

    lax.fori_loop(0, ts // NORM_ROWS, norm_body, 0)


def _conv_norm(v, sz, w_dw, b_dw, g_cn, b_cn, B, S):
    T, C = v.shape
    taps = w_dw.shape[0]
    ns = C // V7X_LANES
    assert taps - 1 <= CONV_HALO and ns % 2 == 0
    ts = _tile(S, 256)
    assert ts % CONV_HALO == 0 and ts % CONV_ROWS == 0 and ts % NORM_ROWS == 0
    per_b = S // ts
    hb = ts // CONV_HALO
    v3 = v.reshape(B, S, C)
    w_slab = w_dw.reshape(taps, ns, V7X_LANES).transpose(1, 0, 2)
    b_slab = b_dw.reshape(ns, 1, V7X_LANES)
    kern = functools.partial(_conv_kernel, taps=taps, ts=ts, C=C)
    row = lambda a: a.reshape(1, C)
    return pl.pallas_call(
        kern,
        grid=(B, per_b),
        in_specs=[
            pl.BlockSpec((1, CONV_HALO, C), lambda b, s: (b, jnp.maximum(s * hb - 1, 0), 0)),
            pl.BlockSpec((1, ts, C), lambda b, s: (b, s, 0)),
            pl.BlockSpec((ts, C), lambda b, s: (b * per_b + s, 0)),
            pl.BlockSpec((ns, taps, V7X_LANES), lambda b, s: (0, 0, 0)),
            pl.BlockSpec((ns, 1, V7X_LANES), lambda b, s: (0, 0, 0)),
            pl.BlockSpec((1, C), lambda b, s: (0, 0)),
            pl.BlockSpec((1, C), lambda b, s: (0, 0)),
        ],
        out_specs=pl.BlockSpec((ts, C), lambda b, s: (b * per_b + s, 0)),
        out_shape=jax.ShapeDtypeStruct((T, C), BF16),
        scratch_shapes=[pltpu.VMEM((ns // 2, 2 * (ts + CONV_HALO), V7X_LANES), F32),
                        pltpu.VMEM((ns, ts, V7X_LANES), F32),
                        pltpu.VMEM((ts, V7X_LANES), F32), pltpu.VMEM((ts, V7X_LANES), F32)],
        compiler_params=_params("parallel", "parallel"),
        name="conv_norm",
    )(v3, v3, sz, w_slab, b_slab, row(g_cn), row(b_cn))


OUT_NORM_TK = 4096


def _out_kernel(act_ref, w_ref, x_ref, mod_ref, g_ref, b_ref, *rest, alpha, nk, nj, tn, D, with_next):
    if with_next:
        modn_ref, o_ref, h_ref, y_ref = rest
    else:
        o_ref, y_ref = rest
    kk = pl.program_id(1)
    j = pl.program_id(2)
    out = _dot(act_ref[...], w_ref[...])

    def residual(total):
        gate = 1.0 + mod_ref[0, 2:3, :]
        y_ref[j] = alpha * x_ref[...] + gate * total

    if nk == 1:
        residual(out)
    else:
        @pl.when(kk == 0)
        def _():
            y_ref[j] = out

        @pl.when(jnp.logical_and(kk > 0, kk < nk - 1))
        def _():
            y_ref[j] += out

        @pl.when(kk == nk - 1)
        def _():
            residual(y_ref[j] + out)

    @pl.when(jnp.logical_and(kk == nk - 1, j == nj - 1))
    def _():
        tot = jnp.sum(y_ref[0], axis=-1, keepdims=True)
        for t in range(1, nj):
            tot = tot + jnp.sum(y_ref[t], axis=-1, keepdims=True)
        mu = tot * (1.0 / D)
        sq = jnp.zeros_like(mu)
        for t in range(nj):
            d = y_ref[t] - mu
            sq = sq + jnp.sum(d * d, axis=-1, keepdims=True)
        rs = lax.rsqrt(sq * (1.0 / D) + LN_EPS)
        for t in range(nj):
            cs = slice(t * tn, (t + 1) * tn)
            o_ref[:, cs] = (y_ref[t] - mu) * rs * g_ref[:, cs] + b_ref[:, cs]
        if with_next:
            tot = jnp.sum(o_ref[:, 0:tn], axis=-1, keepdims=True)
            for t in range(1, nj):
                tot = tot + jnp.sum(o_ref[:, t * tn:(t + 1) * tn], axis=-1, keepdims=True)
            mu = tot * (1.0 / D)
            sq = jnp.zeros_like(mu)
            for t in range(nj):
                d = o_ref[:, t * tn:(t + 1) * tn] - mu
                sq = sq + jnp.sum(d * d, axis=-1, keepdims=True)
            rs = lax.rsqrt(sq * (1.0 / D) + LN_EPS)
            for t in range(nj):
                cs = slice(t * tn, (t + 1) * tn)
                hn = (o_ref[:, cs] - mu) * rs
                h_ref[:, cs] = (hn * (1.0 + modn_ref[0, 1:2, cs]) + modn_ref[0, 0:1, cs]).astype(BF16)


def _out_norm(act, w_out, x2, mod, ln_g, ln_b, alpha, S, mod_next=None):
    T, K = act.shape
    D = w_out.shape[1]
    tm = _tile(S, 512)
    tn = _tile(D, 512)
    tk = _tile(K, OUT_NORM_TK)
    nj = D // tn
    nk = K // tk
    per_b = S // tm
    with_next = mod_next is not None
    kern = functools.partial(_out_kernel, alpha=alpha, nk=nk, nj=nj, tn=tn, D=D,
                             with_next=with_next)
    row = pl.BlockSpec((tm, D), lambda i, k, j: (i, 0))
    in_specs = [
        pl.BlockSpec((tm, tk), lambda i, k, j: (i, k)),
        pl.BlockSpec((tk, tn), lambda i, k, j: (k, j)),
        pl.BlockSpec((tm, tn), lambda i, k, j: (i, j)),
        pl.BlockSpec((1, 3, tn), lambda i, k, j: (i // per_b, 0, j)),
        pl.BlockSpec((1, D), lambda i, k, j: (0, 0)),
        pl.BlockSpec((1, D), lambda i, k, j: (0, 0)),
    ]
    args = [act, w_out, x2, mod, ln_g.reshape(1, D), ln_b.reshape(1, D)]
    out_specs, out_shape = row, jax.ShapeDtypeStruct((T, D), F32)
    if with_next:
        in_specs.append(pl.BlockSpec((1, 3, D), lambda i, k, j: (i // per_b, 0, 0)))
        args.append(mod_next)
        out_specs = [row, row]
        out_shape = [out_shape, jax.ShapeDtypeStruct((T, D), BF16)]
    return pl.pallas_call(
        kern,
        grid=(T // tm, nk, nj),
        in_specs=in_specs,
        out_specs=out_specs,
        out_shape=out_shape,
        scratch_shapes=[pltpu.VMEM((nj, tm, tn), F32)],
        compiler_params=_params("parallel", "arbitrary", "arbitrary"),
        name="out_norm",
    )(*args)


K_ROW_LANES = 2 * V7X_LANES


def _kv_kernel(x_ref, wa_ref, g_ref, cs_ref, wk_ref, wvt_ref, k_ref, vt_ref, ckv_ref, kr_ref,
               *, R, hb):
    j = pl.program_id(1)

    @pl.when(j == 0)
    def _():
        kva = _dot(x_ref[...].astype(BF16), wa_ref[...])
        ckv = kva[:, :R]
        ms = jnp.mean(ckv * ckv, axis=-1, keepdims=True)
        ckv_ref[...] = (ckv * lax.rsqrt(ms + RMS_EPS) * g_ref[...]).astype(BF16)
        u = kva[:, R:] * cs_ref[...]
        kr_ref[...] = (u[:, :QK_ROPE_DIM] + u[:, QK_ROPE_DIM:]).astype(BF16)

    qk = QK_NOPE_DIM + QK_ROPE_DIM
    ckv = ckv_ref[...]
    kn = _dot(ckv, wk_ref[...])
    vt = lax.dot_general(wvt_ref[...], ckv, (((1,), (1,)), ((), ())), preferred_element_type=F32)
    for h in range(hb):
        k_ref[0, h, :, 0:QK_NOPE_DIM] = kn[:, h * QK_NOPE_DIM:(h + 1) * QK_NOPE_DIM].astype(BF16)
        k_ref[0, h, :, QK_NOPE_DIM:qk] = kr_ref[...]
        k_ref[0, h, :, qk:] = jnp.zeros((k_ref.shape[2], K_ROW_LANES - qk), BF16)
        vt_ref[0, h] = vt[h * V_HEAD_DIM:(h + 1) * V_HEAD_DIM, :].astype(BF16)


def _kv_proj(x2, w_kva_ext, g_kv, cs_tab, w_k, w_vt, B, S, H):
    T, D = x2.shape
    R = g_kv.shape[0]
    tm = _tile(S, 512)
    hb = _tile(H, 16)
    per_b = S // tm
    qk = QK_NOPE_DIM + QK_ROPE_DIM
    kern = functools.partial(_kv_kernel, R=R, hb=hb)
    return pl.pallas_call(
        kern,
        grid=(T // tm, H // hb),
        in_specs=[
            pl.BlockSpec((tm, D), lambda i, j: (i, 0)),
            pl.BlockSpec((D, R + 2 * QK_ROPE_DIM), lambda i, j: (0, 0)),
            pl.BlockSpec((1, R), lambda i, j: (0, 0)),
            pl.BlockSpec((tm, 2 * QK_ROPE_DIM), lambda i, j: (i % per_b, 0)),
            pl.BlockSpec((R, hb * QK_NOPE_DIM), lambda i, j: (0, j)),
            pl.BlockSpec((hb * V_HEAD_DIM, R), lambda i, j: (j, 0)),
        ],
        out_specs=[
            pl.BlockSpec((1, hb, tm, K_ROW_LANES), lambda i, j: (i // per_b, j, i % per_b, 0)),
            pl.BlockSpec((1, hb, V_HEAD_DIM, tm), lambda i, j: (i // per_b, j, 0, i % per_b)),
        ],
        out_shape=[jax.ShapeDtypeStruct((B, H, S, K_ROW_LANES), BF16),
                   jax.ShapeDtypeStruct((B, H, V_HEAD_DIM, S), BF16)],
        scratch_shapes=[pltpu.VMEM((tm, R), BF16), pltpu.VMEM((tm, QK_ROPE_DIM), BF16)],
        compiler_params=_params("parallel", "arbitrary"),
        name="kv_proj",
    )(x2, w_kva_ext, g_kv.reshape(1, R), cs_tab, w_k, w_vt)


def _q_kernel(h_ref, wc_ref, g_ref, cos_ref, sin_ref, wqt_ref, q_ref, cq_ref, *, hb, scale):
    j = pl.program_id(1)

    @pl.when(j == 0)
    def _():
        cq = _dot(h_ref[...], wc_ref[...])
        ms = jnp.mean(cq * cq, axis=-1, keepdims=True)
        cq_ref[...] = (cq * lax.rsqrt(ms + RMS_EPS) * g_ref[...]).astype(BF16)

    qt = lax.dot_general(wqt_ref[...], cq_ref[...], (((1,), (1,)), ((), ())),
                         preferred_element_type=F32) * scale
    qk = QK_NOPE_DIM + QK_ROPE_DIM
    half = QK_ROPE_DIM // 2
    cos = cos_ref[...]
    sin = sin_ref[...]
    for h in range(hb):
        r0 = h * qk
        x1 = qt[r0 + QK_NOPE_DIM:r0 + QK_NOPE_DIM + half, :]
        x2 = qt[r0 + QK_NOPE_DIM + half:r0 + qk, :]
        q_ref[0, h, 0:QK_NOPE_DIM, :] = qt[r0:r0 + QK_NOPE_DIM, :].astype(BF16)
        q_ref[0, h, QK_NOPE_DIM:QK_NOPE_DIM + half, :] = (x1 * cos - x2 * sin).astype(BF16)
        q_ref[0, h, QK_NOPE_DIM + half:qk, :] = (x1 * sin + x2 * cos).astype(BF16)


def _q_proj(h, w_cq, g_q, cos_t, sin_t, w_qt, B, S, H, scale):
    T, D = h.shape
    R = g_q.shape[0]
    tm = _tile(S, 512)
    hb = _tile(H, 16)
    per_b = S // tm
    qk = QK_NOPE_DIM + QK_ROPE_DIM
    half = QK_ROPE_DIM // 2
    kern = functools.partial(_q_kernel, hb=hb, scale=scale)
    return pl.pallas_call(
        kern,
        grid=(T // tm, H // hb),
        in_specs=[
            pl.BlockSpec((tm, D), lambda i, j: (i, 0)),
            pl.BlockSpec((D, R), lambda i, j: (0, 0)),
            pl.BlockSpec((1, R), lambda i, j: (0, 0)),
            pl.BlockSpec((half, tm), lambda i, j: (0, i % per_b)),
            pl.BlockSpec((half, tm), lambda i, j: (0, i % per_b)),
            pl.BlockSpec((hb * qk, R), lambda i, j: (j, 0)),
        ],
        out_specs=pl.BlockSpec((1, hb, qk, tm), lambda i, j: (i // per_b, j, 0, i % per_b)),
        out_shape=jax.ShapeDtypeStruct((B, H, qk, S), BF16),
        scratch_shapes=[pltpu.VMEM((tm, R), BF16)],
        compiler_params=_params("parallel", "arbitrary"),
        name="q_proj",
    )(h, w_cq, g_q.reshape(1, R), cos_t, sin_t, w_qt)


def _silu_proj_kernel(h_ref, w_ref, o_ref):
    o_ref[...] = _silu(_dot(h_ref[...], w_ref[...]))


def _silu_proj(h, w_in, col0):
    T, D = h.shape
    N = w_in.shape[1] - col0
    tm = _tile(T, 1024)
    tn = _tile(math.gcd(N, col0), 1024)
    c0 = col0 // tn
    return pl.pallas_call(
        _silu_proj_kernel,
        grid=(T // tm, N // tn),
        in_specs=[
            pl.BlockSpec((tm, D), lambda i, j: (i, 0)),
            pl.BlockSpec((D, tn), lambda i, j: (0, j + c0)),
        ],
        out_specs=pl.BlockSpec((tm, tn), lambda i, j: (i, j)),
        out_shape=jax.ShapeDtypeStruct((T, N), F32),
        compiler_params=_params("parallel", "arbitrary"),
        name="silu_proj",
    )(h, w_in)


ATTN_HEADS_PER_STEP = 4
ATTN_QBLOCKS_PER_STEP = 4
ATTN_EXP_ROWS = 16


def _attn_kernel(q_ref, k_ref, vt_ref, sz_ref, bias_ref, o_ref,
                 m_ref, l_ref, acc_ref, s_ref, p_ref, mb_ref, al_ref, *, tq, tk, G, QB):
    q0 = pl.program_id(2) * QB
    m_ref[...] = jnp.full(m_ref.shape, -jnp.inf, F32)
    l_ref[...] = jnp.zeros(l_ref.shape, F32)
    acc_ref[...] = jnp.zeros(acc_ref.shape, F32)
    lg = G.bit_length() - 1
    ch = ATTN_EXP_ROWS

    def first_item(qb):
        return G * (qb * (q0 + 1) + ((qb * (qb - 1)) >> 1))

    n = first_item(QB)

    def item(it):
        qb = 0
        for u in range(1, QB):
            qb = qb + (it >= first_item(u)).astype(jnp.int32)
        local = it - first_item(qb)
        g = jnp.bitwise_and(local, G - 1)
        ki = jnp.right_shift(local, lg)
        return qb, g, ki, pl.multiple_of(ki * tk, tk)

    def scores(it, slot):
        qb, g, ki, k0 = item(it)
        st = qb * G + g
        q = q_ref[0, g, :, pl.ds(pl.multiple_of(qb * tq, tq), tq)]
        s = _dot(k_ref[0, g, pl.ds(k0, tk), 0:q.shape[0]], q)
        s = s + bias_ref[(ki == q0 + qb).astype(jnp.int32)]
        s_ref[slot] = s
        mx = jnp.max(s.reshape(tk // SUBLANES, SUBLANES, tq), axis=0)
        m_prev = m_ref[st]
        m_new = jnp.maximum(m_prev, jnp.max(mx, axis=0, keepdims=True))
        m_ref[st] = m_new
        mb_ref[slot] = m_new
        al_ref[jnp.bitwise_and(it, 3)] = jnp.exp2(m_prev - m_new)

    def softmax(it, slot):
        qb, g, _, _ = item(it)
        st = qb * G + g
        mb = jnp.broadcast_to(mb_ref[slot], (ch, tq))
        ls = None
        for c in range(tk // ch):
            p = jnp.exp2(s_ref[slot, c * ch:(c + 1) * ch, :] - mb)
            ls = p if ls is None else ls + p
            p_ref[slot, c * ch:(c + 1) * ch, :] = p.astype(BF16)
        alpha = al_ref[jnp.bitwise_and(it, 3)]
        l_ref[st] = alpha * l_ref[st] + jnp.sum(ls, axis=0, keepdims=True)

    def accumulate(it, slot):
        qb, g, _, k0 = item(it)
        st = qb * G + g
        pv = _dot(vt_ref[0, g, :, pl.ds(k0, tk)], p_ref[slot])
        acc_ref[st] = al_ref[jnp.bitwise_and(it, 3)] * acc_ref[st] + pv

    scores(0, 0)
    scores(1, 1)
    softmax(0, 0)

    def steady(t, slot):
        accumulate(t - 2, slot)
        softmax(t - 1, 1 - slot)
        scores(t, slot)

    def body(j, carry):
        steady(2 * j, 0)
        steady(2 * j + 1, 1)
        return carry

    lax.fori_loop(1, n // 2, body, 0)
    accumulate(n - 2, 0)
    softmax(n - 1, 1)
    accumulate(n - 1, 1)

    for qb in range(QB):
        rows = slice(qb * tq, (qb + 1) * tq)
        for g in range(G):
            cs = slice(g * V_HEAD_DIM, (g + 1) * V_HEAD_DIM)
            o = (acc_ref[qb * G + g] / l_ref[qb * G + g]).T
            o_ref[rows, cs] = (o * sz_ref[rows, cs]).astype(BF16)


def _attention(q, k, vt, sz, B, S, H):
    tq = tk = _tile(S, 512)
    G = _tile(H, ATTN_HEADS_PER_STEP)
    assert G >= 2 and G & (G - 1) == 0
    nq = S // tq
    QB = _tile(nq, ATTN_QBLOCKS_PER_STEP)
    ns = nq // QB
    qk = q.shape[2]
    r = lax.broadcasted_iota(jnp.int32, (tk, tq), 0)
    c = lax.broadcasted_iota(jnp.int32, (tk, tq), 1)
    bias = jnp.stack([jnp.zeros((tk, tq), F32), jnp.where(c >= r, 0.0, -jnp.inf).astype(F32)])
    kern = functools.partial(_attn_kernel, tq=tq, tk=tk, G=G, QB=QB)
    return pl.pallas_call(
        kern,
        grid=(B, H // G, ns),
        in_specs=[
            pl.BlockSpec((1, G, qk, QB * tq), lambda b, h, i: (b, h, 0, i)),
            pl.BlockSpec((1, G, S, k.shape[-1]), lambda b, h, i: (b, h, 0, 0)),
            pl.BlockSpec((1, G, V_HEAD_DIM, S), lambda b, h, i: (b, h, 0, 0)),
            pl.BlockSpec((QB * tq, G * V_HEAD_DIM), lambda b, h, i: (b * ns + i, h)),
            pl.BlockSpec((2, tk, tq), lambda b, h, i: (0, 0, 0)),
        ],
        out_specs=pl.BlockSpec((QB * tq, G * V_HEAD_DIM), lambda b, h, i: (b * ns + i, h)),
        out_shape=jax.ShapeDtypeStruct((B * S, H * V_HEAD_DIM), BF16),
        scratch_shapes=[pltpu.VMEM((QB * G, 1, tq), F32), pltpu.VMEM((QB * G, 1, tq), F32),
                        pltpu.VMEM((QB * G, V_HEAD_DIM, tq), F32),
                        pltpu.VMEM((2, tk, tq), F32), pltpu.VMEM((2, tk, tq), BF16),
                        pltpu.VMEM((2, 1, tq), F32), pltpu.VMEM((4, 1, tq), F32)],
        compiler_params=_params("parallel", "parallel", "arbitrary"),
        name="attention",
    )(q, k, vt, sz, bias)


def _rope_tables(S):
    half = QK_ROPE_DIM // 2
    inv_freq = ROPE_BASE ** (-jnp.arange(half, dtype=F32) / half)
    ang = jnp.arange(S, dtype=jnp.int32).astype(F32)[:, None] * inv_freq[None, :]
    return jnp.cos(ang), jnp.sin(ang)


def kernel(x, c, w_ada, b_ada, ln_g, ln_b, a_w_in, a_w_dw, a_b_dw, a_norm_g, a_norm_b, a_w_out,
           b_w_in, b_q_norm_g, b_w_qb, b_w_out, kv_w_a, kv_norm_g, kv_w_b):
    B, S, D = x.shape
    depth = w_ada.shape[0]
    n_a = a_w_in.shape[0]
    alpha = (2.0 * depth) ** 0.25
    H = b_w_out.shape[1] // V_HEAD_DIM
    q_rank = b_q_norm_g.shape[1]
    kv_rank = kv_norm_g.shape[0]
    qk = QK_NOPE_DIM + QK_ROPE_DIM
    half = QK_ROPE_DIM // 2

    mod = _ada_mod(c, w_ada, b_ada)
    cos, sin = _rope_tables(S)
    k_tab = jnp.concatenate([cos, cos, -sin, sin], axis=1)

    x2 = x.reshape(B * S, D)
    kv = None
    h = None
    for layer in range(depth):
        if layer == n_a:
            rope_w = kv_w_a[:, kv_rank:]
            w_kva_ext = jnp.concatenate(
                [kv_w_a, rope_w[:, half:], rope_w[:, :half]], axis=1).astype(BF16)
            w_kvb = kv_w_b.reshape(kv_rank, H, QK_NOPE_DIM + V_HEAD_DIM)
            w_k = w_kvb[:, :, :QK_NOPE_DIM].reshape(kv_rank, H * QK_NOPE_DIM).astype(BF16)
            w_vt = w_kvb[:, :, QK_NOPE_DIM:].reshape(kv_rank, H * V_HEAD_DIM).T.astype(BF16)
            kv = _kv_proj(x2, w_kva_ext, kv_norm_g, k_tab, w_k, w_vt, B, S, H)
        if h is None:
            h = _ln_modulate(x2, mod[layer], S)
        if layer < n_a:
            i = layer
            v, sz = _glu_proj(h, a_w_in[i].astype(BF16))
            act = _conv_norm(v, sz, a_w_dw[i], a_b_dw[i], a_norm_g[i], a_norm_b[i], B, S)
            w_out = a_w_out[i]
        else:
            j = layer - n_a
            w_in = b_w_in[j]
            w_in = w_in.astype(BF16)
            q = _q_proj(h, w_in, b_q_norm_g[j], cos.T, sin.T,
                        b_w_qb[j].T.astype(BF16), B, S, H, qk ** -0.5 * math.log2(math.e))
            sz = _silu_proj(h, w_in, q_rank)
            act = _attention(q, kv[0], kv[1], sz, B, S, H)
            w_out = b_w_out[j]
        w_out = w_out.astype(BF16)
        if layer + 1 < depth:
            x2, h = _out_norm(act, w_out, x2, mod[layer], ln_g[layer], ln_b[layer], alpha, S,
                              mod_next=mod[layer + 1])
        else:
            x2 = _out_norm(act, w_out, x2, mod[layer], ln_g[layer], ln_b[layer], alpha, S)
    return x2.reshape(B, S, D)
```

```python
import functools
import math

import jax
import jax.numpy as jnp
from jax import lax
from jax.experimental import pallas as pl
from jax.experimental.pallas import tpu as pltpu

QK_NOPE_DIM = 128
QK_ROPE_DIM = 64
V_HEAD_DIM = 128
ROPE_BASE = 10000.0
LN_EPS = 1e-5
RMS_EPS = 1e-6

V7X_LANES = 128
V7X_VMEM_BYTES = 64 * 1024 * 1024
VMEM_LIMIT_BYTES = 60 * 1024 * 1024

F32 = jnp.float32
BF16 = jnp.bfloat16


def _params(*semantics):
    return pltpu.CompilerParams(dimension_semantics=semantics,
                                vmem_limit_bytes=VMEM_LIMIT_BYTES)


def _tile(dim, want):
    t = min(dim, want)
    while dim % t:
        t //= 2
    return t


def _dot(a, b):
    return jnp.dot(a, b, preferred_element_type=F32)


def _silu(x):
    return x * jax.nn.sigmoid(x)


def _ada_kernel(c_ref, w_ref, b_ref, o_ref):
    sc = _silu(c_ref[...]).astype(BF16)
    o_ref[0] = _dot(sc, w_ref[0].astype(BF16)) + b_ref[0]


def _ada_mod(c, w_ada, b_ada):
    L, D, N = w_ada.shape
    B = c.shape[0]
    Bp = max(8, B)
    cp = jnp.pad(c, ((0, Bp - B), (0, 0)))
    tn = _tile(N, 512)
    out = pl.pallas_call(
        _ada_kernel,
        grid=(L, N // tn),
        in_specs=[
            pl.BlockSpec((Bp, D), lambda l, j: (0, 0)),
            pl.BlockSpec((1, D, tn), lambda l, j: (l, 0, j)),
            pl.BlockSpec((1, 1, tn), lambda l, j: (l, 0, j)),
        ],
        out_specs=pl.BlockSpec((1, Bp, tn), lambda l, j: (l, 0, j)),
        out_shape=jax.ShapeDtypeStruct((L, Bp, N), F32),
        compiler_params=_params("parallel", "parallel"),
        name="ada_mod",
    )(cp, w_ada, b_ada.reshape(L, 1, N))
    return out[:, :B].reshape(L, B, 3, D)


def _ln_mod_kernel(x_ref, mod_ref, h_ref):
    x = x_ref[...]
    mu = jnp.mean(x, axis=-1, keepdims=True)
    xc = x - mu
    var = jnp.mean(xc * xc, axis=-1, keepdims=True)
    y = xc * lax.rsqrt(var + LN_EPS)
    shift = mod_ref[0, 0:1, :]
    scale = mod_ref[0, 1:2, :]
    h_ref[...] = (y * (1.0 + scale) + shift).astype(BF16)


def _ln_modulate(x2, mod, S):
    T, D = x2.shape
    tm = _tile(S, 256)
    per_b = S // tm
    return pl.pallas_call(
        _ln_mod_kernel,
        grid=(T // tm,),
        in_specs=[
            pl.BlockSpec((tm, D), lambda i: (i, 0)),
            pl.BlockSpec((1, 3, D), lambda i: (i // per_b, 0, 0)),
        ],
        out_specs=pl.BlockSpec((tm, D), lambda i: (i, 0)),
        out_shape=jax.ShapeDtypeStruct((T, D), BF16),
        compiler_params=_params("parallel"),
        name="ln_modulate",
    )(x2, mod)


def _glu_kernel(h_ref, wa_ref, wg_ref, wz_ref, v_ref, sz_ref):
    h = h_ref[...]
    a = _dot(h, wa_ref[...])
    g = _dot(h, wg_ref[...])
    v_ref[...] = a * jax.nn.sigmoid(g)
    z = _dot(h, wz_ref[...])
    sz_ref[...] = _silu(z)


def _glu_proj(h, w_in):
    T, D = h.shape
    C = w_in.shape[1] // 3
    tm = _tile(T, 1024)
    tn = _tile(C, 512)
    nj = C // tn
    return pl.pallas_call(
        _glu_kernel,
        grid=(T // tm, nj),
        in_specs=[
            pl.BlockSpec((tm, D), lambda i, j: (i, 0)),
            pl.BlockSpec((D, tn), lambda i, j: (0, j)),
            pl.BlockSpec((D, tn), lambda i, j: (0, j + nj)),
            pl.BlockSpec((D, tn), lambda i, j: (0, j + 2 * nj)),
        ],
        out_specs=[
            pl.BlockSpec((tm, tn), lambda i, j: (i, j)),
            pl.BlockSpec((tm, tn), lambda i, j: (i, j)),
        ],
        out_shape=[jax.ShapeDtypeStruct((T, C), F32), jax.ShapeDtypeStruct((T, C), F32)],
        compiler_params=_params("parallel", "arbitrary"),
        name="glu_proj",
    )(h, w_in, w_in, w_in)


CONV_HALO = 32
CONV_ROWS = 64
NORM_ROWS = 16
SUBLANES = 8


def _conv_kernel(vh_ref, vm_ref, sz_ref, w_ref, bdw_ref, g_ref, b_ref, o_ref,
                 slab_ref, cv_ref, mu_ref, rs_ref,
                 *, taps, ts, C):
    s = pl.program_id(1)
    ns = C // V7X_LANES
    lead = CONV_HALO - (taps - 1)
    for sl in range(ns):
        p, par = divmod(sl, 2)
        ls = slice(sl * V7X_LANES, (sl + 1) * V7X_LANES)
        halo = vh_ref[0, :, ls]
        slab_ref[p, pl.ds(par, CONV_HALO, stride=2), :] = jnp.where(s > 0, halo, jnp.zeros_like(halo))
        slab_ref[p, pl.ds(2 * CONV_HALO + par, ts, stride=2), :] = vm_ref[0, :, ls]

    nacc = CONV_ROWS // SUBLANES

    def slab_body(sl, carry):
        p = sl // 2
        par = sl % 2
        wk = [jnp.broadcast_to(w_ref[sl, k:k + 1, :], (SUBLANES, V7X_LANES)) for k in range(taps)]
        bias = bdw_ref[sl]

        def row_body(rb, c2):
            r0 = rb * CONV_ROWS
            accs = [None] * nacc
            for o in range(lead, lead + taps + CONV_ROWS - SUBLANES):
                x = slab_ref[p, pl.ds(2 * (r0 + o) + par, SUBLANES, stride=2), :]
                for i in range(nacc):
                    k = o - lead - SUBLANES * i
                    if 0 <= k < taps:
                        t = x * wk[k]
                        accs[i] = t if accs[i] is None else accs[i] + t
            for i in range(nacc):
                rows = pl.ds(pl.multiple_of(r0 + SUBLANES * i, SUBLANES), SUBLANES)
                cv_ref[sl, rows, :] = accs[i] + bias
            return c2

        lax.fori_loop(0, ts // CONV_ROWS, row_body, 0)
        return carry

    lax.fori_loop(0, ns, slab_body, 0)

    def stats_body(rb, carry):
        rows = pl.ds(pl.multiple_of(rb * CONV_ROWS, CONV_ROWS), CONV_ROWS)
        tot = cv_ref[0, rows, :]
        for sl in range(1, ns):
            tot = tot + cv_ref[sl, rows, :]
        mu = jnp.broadcast_to(jnp.sum(tot, axis=-1, keepdims=True) * (1.0 / C),
                              (CONV_ROWS, V7X_LANES))
        sq = None
        for sl in range(ns):
            d = cv_ref[sl, rows, :] - mu
            sq = d * d if sq is None else sq + d * d
        var = jnp.sum(sq, axis=-1, keepdims=True) * (1.0 / C)
        mu_ref[rows, :] = mu
        rs_ref[rows, :] = jnp.broadcast_to(lax.rsqrt(var + LN_EPS), (CONV_ROWS, V7X_LANES))
        return carry

    lax.fori_loop(0, ts // CONV_ROWS, stats_body, 0)

    def norm_body(rg, carry):
        rows = pl.ds(pl.multiple_of(rg * NORM_ROWS, NORM_ROWS), NORM_ROWS)
        mu = mu_ref[rows, :]
        rs = rs_ref[rows, :]
        for sl in range(ns):
            ls = slice(sl * V7X_LANES, (sl + 1) * V7X_LANES)
            yn = (cv_ref[sl, rows, :] - mu) * rs * g_ref[:, ls] + b_ref[:, ls]
            o_ref[rows, ls] = (_silu(yn) * sz_ref[rows, ls]).astype(BF16)
        return carry

    lax.fori_loop(0, ts // NORM_ROWS, norm_body, 0)


def _conv_norm(v, sz, w_dw, b_dw, g_cn, b_cn, B, S):
    T, C = v.shape
    taps = w_dw.shape[0]
    ns = C // V7X_LANES
    assert taps - 1 <= CONV_HALO and ns % 2 == 0
    ts = _tile(S, 256)
    assert ts % CONV_HALO == 0 and ts % CONV_ROWS == 0 and ts % NORM_ROWS == 0
    per_b = S // ts
    hb = ts // CONV_HALO
    v3 = v.reshape(B, S, C)
    w_slab = w_dw.reshape(taps, ns, V7X_LANES).transpose(1, 0, 2)
    b_slab = b_dw.reshape(ns, 1, V7X_LANES)
    kern = functools.partial(_conv_kernel, taps=taps, ts=ts, C=C)
    row = lambda a: a.reshape(1, C)
    return pl.pallas_call(
        kern,
        grid=(B, per_b),
        in_specs=[
            pl.BlockSpec((1, CONV_HALO, C), lambda b, s: (b, jnp.maximum(s * hb - 1, 0), 0)),
            pl.BlockSpec((1, ts, C), lambda b, s: (b, s, 0)),
            pl.BlockSpec((ts, C), lambda b, s: (b * per_b + s, 0)),
            pl.BlockSpec((ns, taps, V7X_LANES), lambda b, s: (0, 0, 0)),
            pl.BlockSpec((ns, 1, V7X_LANES), lambda b, s: (0, 0, 0)),
            pl.BlockSpec((1, C), lambda b, s: (0, 0)),
            pl.BlockSpec((1, C), lambda b, s: (0, 0)),
        ],
        out_specs=pl.BlockSpec((ts, C), lambda b, s: (b * per_b + s, 0)),
        out_shape=jax.ShapeDtypeStruct((T, C), BF16),
        scratch_shapes=[pltpu.VMEM((ns // 2, 2 * (ts + CONV_HALO), V7X_LANES), F32),
                        pltpu.VMEM((ns, ts, V7X_LANES), F32),
                        pltpu.VMEM((ts, V7X_LANES), F32), pltpu.VMEM((ts, V7X_LANES), F32)],
        compiler_params=_params("parallel", "parallel"),
        name="conv_norm",
    )(v3, v3, sz, w_slab, b_slab, row(g_cn), row(b_cn))


OUT_NORM_TK = 8192


def _out_kernel(act_ref, w_ref, x_ref, mod_ref, g_ref, b_ref, *rest, alpha, nk, nj, tn, D, with_next):
    if with_next:
        modn_ref, o_ref, h_ref, y_ref = rest
    else:
        o_ref, y_ref = rest
    kk = pl.program_id(1)
    j = pl.program_id(2)
    out = _dot(act_ref[...], w_ref[...])

    def residual(total):
        gate = 1.0 + mod_ref[0, 2:3, :]
        y_ref[j] = alpha * x_ref[...] + gate * total

    if nk == 1:
        residual(out)
    else:
        @pl.when(kk == 0)
        def _():
            y_ref[j] = out

        @pl.when(jnp.logical_and(kk > 0, kk < nk - 1))
        def _():
            y_ref[j] += out

        @pl.when(kk == nk - 1)
        def _():
            residual(y_ref[j] + out)

    @pl.when(jnp.logical_and(kk == nk - 1, j == nj - 1))
    def _():
        tot = jnp.sum(y_ref[0], axis=-1, keepdims=True)
        for t in range(1, nj):
            tot = tot + jnp.sum(y_ref[t], axis=-1, keepdims=True)
        mu = tot * (1.0 / D)
        sq = jnp.zeros_like(mu)
        for t in range(nj):
            d = y_ref[t] - mu
            sq = sq + jnp.sum(d * d, axis=-1, keepdims=True)
        rs = lax.rsqrt(sq * (1.0 / D) + LN_EPS)
        for t in range(nj):
            cs = slice(t * tn, (t + 1) * tn)
            o_ref[:, cs] = (y_ref[t] - mu) * rs * g_ref[:, cs] + b_ref[:, cs]
        if with_next:
            tot = jnp.sum(o_ref[:, 0:tn], axis=-1, keepdims=True)
            for t in range(1, nj):
                tot = tot + jnp.sum(o_ref[:, t * tn:(t + 1) * tn], axis=-1, keepdims=True)
            mu = tot * (1.0 / D)
            sq = jnp.zeros_like(mu)
            for t in range(nj):
                d = o_ref[:, t * tn:(t + 1) * tn] - mu
                sq = sq + jnp.sum(d * d, axis=-1, keepdims=True)
            rs = lax.rsqrt(sq * (1.0 / D) + LN_EPS)
            for t in range(nj):
                cs = slice(t * tn, (t + 1) * tn)
                hn = (o_ref[:, cs] - mu) * rs
                h_ref[:, cs] = (hn * (1.0 + modn_ref[0, 1:2, cs]) + modn_ref[0, 0:1, cs]).astype(BF16)


def _out_norm(act, w_out, x2, mod, ln_g, ln_b, alpha, S, mod_next=None):
    T, K = act.shape
    D = w_out.shape[1]
    tm = _tile(S, 512)
    tn = _tile(D, 512 if K <= 4096 else 256)
    tk = _tile(K, OUT_NORM_TK)
    nj = D // tn
    nk = K // tk
    per_b = S // tm
    with_next = mod_next is not None
    kern = functools.partial(_out_kernel, alpha=alpha, nk=nk, nj=nj, tn=tn, D=D,
                             with_next=with_next)
    row = pl.BlockSpec((tm, D), lambda i, k, j: (i, 0))
    in_specs = [
        pl.BlockSpec((tm, tk), lambda i, k, j: (i, k)),
        pl.BlockSpec((tk, tn), lambda i, k, j: (k, j)),
        pl.BlockSpec((tm, tn), lambda i, k, j: (i, j)),
        pl.BlockSpec((1, 3, tn), lambda i, k, j: (i // per_b, 0, j)),
        pl.BlockSpec((1, D), lambda i, k, j: (0, 0)),
        pl.BlockSpec((1, D), lambda i, k, j: (0, 0)),
    ]
    args = [act, w_out, x2, mod, ln_g.reshape(1, D), ln_b.reshape(1, D)]
    out_specs, out_shape = row, jax.ShapeDtypeStruct((T, D), F32)
    if with_next:
        in_specs.append(pl.BlockSpec((1, 3, D), lambda i, k, j: (i // per_b, 0, 0)))
        args.append(mod_next)
        out_specs = [row, row]
        out_shape = [out_shape, jax.ShapeDtypeStruct((T, D), BF16)]
    return pl.pallas_call(
        kern,
        grid=(T // tm, nk, nj),
        in_specs=in_specs,
        out_specs=out_specs,
        out_shape=out_shape,
        scratch_shapes=[pltpu.VMEM((nj, tm, tn), F32)],
        compiler_params=_params("parallel", "arbitrary", "arbitrary"),
        name="out_norm",
    )(*args)


K_ROW_LANES = 2 * V7X_LANES


def _kv_kernel(x_ref, wa_ref, g_ref, cs_ref, wk_ref, wvt_ref, k_ref, vt_ref, ckv_ref, kr_ref,
               *, R, hb):
    j = pl.program_id(1)

    @pl.when(j == 0)
    def _():
        kva = _dot(x_ref[...].astype(BF16), wa_ref[...])
        ckv = kva[:, :R]
        ms = jnp.mean(ckv * ckv, axis=-1, keepdims=True)
        ckv_ref[...] = (ckv * lax.rsqrt(ms + RMS_EPS) * g_ref[...]).astype(BF16)
        u = kva[:, R:] * cs_ref[...]
        kr_ref[...] = (u[:, :QK_ROPE_DIM] + u[:, QK_ROPE_DIM:]).astype(BF16)

    qk = QK_NOPE_DIM + QK_ROPE_DIM
    ckv = ckv_ref[...]
    kn = _dot(ckv, wk_ref[...])
    vt = lax.dot_general(wvt_ref[...], ckv, (((1,), (1,)), ((), ())), preferred_element_type=F32)
    for h in range(hb):
        k_ref[0, h, :, 0:QK_NOPE_DIM] = kn[:, h * QK_NOPE_DIM:(h + 1) * QK_NOPE_DIM].astype(BF16)
        k_ref[0, h, :, QK_NOPE_DIM:qk] = kr_ref[...]
        k_ref[0, h, :, qk:] = jnp.zeros((k_ref.shape[2], K_ROW_LANES - qk), BF16)
        vt_ref[0, h] = vt[h * V_HEAD_DIM:(h + 1) * V_HEAD_DIM, :].astype(BF16)


def _kv_proj(x2, w_kva_ext, g_kv, cs_tab, w_k, w_vt, B, S, H):
    T, D = x2.shape
    R = g_kv.shape[0]
    tm = _tile(S, 512)
    hb = _tile(H, 16)
    per_b = S // tm
    qk = QK_NOPE_DIM + QK_ROPE_DIM
    kern = functools.partial(_kv_kernel, R=R, hb=hb)
    return pl.pallas_call(
        kern,
        grid=(T // tm, H // hb),
        in_specs=[
            pl.BlockSpec((tm, D), lambda i, j: (i, 0)),
            pl.BlockSpec((D, R + 2 * QK_ROPE_DIM), lambda i, j: (0, 0)),
            pl.BlockSpec((1, R), lambda i, j: (0, 0)),
            pl.BlockSpec((tm, 2 * QK_ROPE_DIM), lambda i, j: (i % per_b, 0)),
            pl.BlockSpec((R, hb * QK_NOPE_DIM), lambda i, j: (0, j)),
            pl.BlockSpec((hb * V_HEAD_DIM, R), lambda i, j: (j, 0)),
        ],
        out_specs=[
            pl.BlockSpec((1, hb, tm, K_ROW_LANES), lambda i, j: (i // per_b, j, i % per_b, 0)),
            pl.BlockSpec((1, hb, V_HEAD_DIM, tm), lambda i, j: (i // per_b, j, 0, i % per_b)),
        ],
        out_shape=[jax.ShapeDtypeStruct((B, H, S, K_ROW_LANES), BF16),
                   jax.ShapeDtypeStruct((B, H, V_HEAD_DIM, S), BF16)],
        scratch_shapes=[pltpu.VMEM((tm, R), BF16), pltpu.VMEM((tm, QK_ROPE_DIM), BF16)],
        compiler_params=_params("parallel", "arbitrary"),
        name="kv_proj",
    )(x2, w_kva_ext, g_kv.reshape(1, R), cs_tab, w_k, w_vt)


def _q_kernel(h_ref, wc_ref, g_ref, cos_ref, sin_ref, wqt_ref, q_ref, cq_ref, *, hb, scale):
    j = pl.program_id(1)

    @pl.when(j == 0)
    def _():
        cq = _dot(h_ref[...], wc_ref[...])
        ms = jnp.mean(cq * cq, axis=-1, keepdims=True)
        cq_ref[...] = (cq * lax.rsqrt(ms + RMS_EPS) * g_ref[...]).astype(BF16)

    qt = lax.dot_general(wqt_ref[...], cq_ref[...], (((1,), (1,)), ((), ())),
                         preferred_element_type=F32) * scale
    qk = QK_NOPE_DIM + QK_ROPE_DIM
    half = QK_ROPE_DIM // 2
    cos = cos_ref[...]
    sin = sin_ref[...]
    for h in range(hb):
        r0 = h * qk
        x1 = qt[r0 + QK_NOPE_DIM:r0 + QK_NOPE_DIM + half, :]
        x2 = qt[r0 + QK_NOPE_DIM + half:r0 + qk, :]
        q_ref[0, h, 0:QK_NOPE_DIM, :] = qt[r0:r0 + QK_NOPE_DIM, :].astype(BF16)
        q_ref[0, h, QK_NOPE_DIM:QK_NOPE_DIM + half, :] = (x1 * cos - x2 * sin).astype(BF16)
        q_ref[0, h, QK_NOPE_DIM + half:qk, :] = (x1 * sin + x2 * cos).astype(BF16)


def _q_proj(h, w_cq, g_q, cos_t, sin_t, w_qt, B, S, H, scale):
    T, D = h.shape
    R = g_q.shape[0]
    tm = _tile(S, 512)
    hb = _tile(H, 16)
    per_b = S // tm
    qk = QK_NOPE_DIM + QK_ROPE_DIM
    half = QK_ROPE_DIM // 2
    kern = functools.partial(_q_kernel, hb=hb, scale=scale)
    return pl.pallas_call(
        kern,
        grid=(T // tm, H // hb),
        in_specs=[
            pl.BlockSpec((tm, D), lambda i, j: (i, 0)),
            pl.BlockSpec((D, R), lambda i, j: (0, 0)),
            pl.BlockSpec((1, R), lambda i, j: (0, 0)),
            pl.BlockSpec((half, tm), lambda i, j: (0, i % per_b)),
            pl.BlockSpec((half, tm), lambda i, j: (0, i % per_b)),
            pl.BlockSpec((hb * qk, R), lambda i, j: (j, 0)),
        ],
        out_specs=pl.BlockSpec((1, hb, qk, tm), lambda i, j: (i // per_b, j, 0, i % per_b)),
        out_shape=jax.ShapeDtypeStruct((B, H, qk, S), BF16),
        scratch_shapes=[pltpu.VMEM((tm, R), BF16)],
        compiler_params=_params("parallel", "arbitrary"),
        name="q_proj",
    )(h, w_cq, g_q.reshape(1, R), cos_t, sin_t, w_qt)


def _silu_proj_kernel(h_ref, w_ref, o_ref):
    o_ref[...] = _silu(_dot(h_ref[...], w_ref[...]))


def _silu_proj(h, w_in, col0):
    T, D = h.shape
    N = w_in.shape[1] - col0
    tm = _tile(T, 1024)
    tn = _tile(math.gcd(N, col0), 1024)
    c0 = col0 // tn
    return pl.pallas_call(
        _silu_proj_kernel,
        grid=(T // tm, N // tn),
        in_specs=[
            pl.BlockSpec((tm, D), lambda i, j: (i, 0)),
            pl.BlockSpec((D, tn), lambda i, j: (0, j + c0)),
        ],
        out_specs=pl.BlockSpec((tm, tn), lambda i, j: (i, j)),
        out_shape=jax.ShapeDtypeStruct((T, N), F32),
        compiler_params=_params("parallel", "arbitrary"),
        name="silu_proj",
    )(h, w_in)


ATTN_HEADS_PER_STEP = 4
ATTN_QBLOCKS_PER_STEP = 4
ATTN_EXP_ROWS = 16


def _attn_kernel(q_ref, k_ref, vt_ref, sz_ref, bias_ref, o_ref,
                 m_ref, l_ref, acc_ref, s_ref, p_ref, mb_ref, al_ref, *, tq, tk, G, QB):
    q0 = pl.program_id(2) * QB
    m_ref[...] = jnp.full(m_ref.shape, -jnp.inf, F32)
    l_ref[...] = jnp.zeros(l_ref.shape, F32)
    acc_ref[...] = jnp.zeros(acc_ref.shape, F32)
    lg = G.bit_length() - 1
    ch = ATTN_EXP_ROWS

    def first_item(qb):
        return G * (qb * (q0 + 1) + ((qb * (qb - 1)) >> 1))

    n = first_item(QB)

    def item(it):
        qb = 0
        for u in range(1, QB):
            qb = qb + (it >= first_item(u)).astype(jnp.int32)
        local = it - first_item(qb)
        g = jnp.bitwise_and(local, G - 1)
        ki = jnp.right_shift(local, lg)
        return qb, g, ki, pl.multiple_of(ki * tk, tk)

    def scores(it, slot):
        qb, g, ki, k0 = item(it)
        st = qb * G + g
        q = q_ref[0, g, :, pl.ds(pl.multiple_of(qb * tq, tq), tq)]
        s = _dot(k_ref[0, g, pl.ds(k0, tk), 0:q.shape[0]], q)
        s = s + bias_ref[(ki == q0 + qb).astype(jnp.int32)]
        s_ref[slot] = s
        mx = jnp.max(s.reshape(tk // SUBLANES, SUBLANES, tq), axis=0)
        m_prev = m_ref[st]
        m_new = jnp.maximum(m_prev, jnp.max(mx, axis=0, keepdims=True))
        m_ref[st] = m_new
        mb_ref[slot] = m_new
        al_ref[jnp.bitwise_and(it, 3)] = jnp.exp2(m_prev - m_new)

    def softmax(it, slot):
        qb, g, _, _ = item(it)
        st = qb * G + g
        mb = jnp.broadcast_to(mb_ref[slot], (ch, tq))
        ls = None
        for c in range(tk // ch):
            p = jnp.exp2(s_ref[slot, c * ch:(c + 1) * ch, :] - mb)
            ls = p if ls is None else ls + p
            p_ref[slot, c * ch:(c + 1) * ch, :] = p.astype(BF16)
        alpha = al_ref[jnp.bitwise_and(it, 3)]
        l_ref[st] = alpha * l_ref[st] + jnp.sum(ls, axis=0, keepdims=True)

    def accumulate(it, slot):
        qb, g, _, k0 = item(it)
        st = qb * G + g
        pv = _dot(vt_ref[0, g, :, pl.ds(k0, tk)], p_ref[slot])
        acc_ref[st] = al_ref[jnp.bitwise_and(it, 3)] * acc_ref[st] + pv

    scores(0, 0)
    scores(1, 1)
    softmax(0, 0)

    def steady(t, slot):
        accumulate(t - 2, slot)
        softmax(t - 1, 1 - slot)
        scores(t, slot)

    def body(j, carry):
        steady(2 * j, 0)
        steady(2 * j + 1, 1)
        return carry

    lax.fori_loop(1, n // 2, body, 0)
    accumulate(n - 2, 0)
    softmax(n - 1, 1)
    accumulate(n - 1, 1)

    for qb in range(QB):
        rows = slice(qb * tq, (qb + 1) * tq)
        for g in range(G):
            cs = slice(g * V_HEAD_DIM, (g + 1) * V_HEAD_DIM)
            o = (acc_ref[qb * G + g] / l_ref[qb * G + g]).T
            o_ref[rows, cs] = (o * sz_ref[rows, cs]).astype(BF16)


def _attention(q, k, vt, sz, B, S, H):
    tq = tk = _tile(S, 512)
    G = _tile(H, ATTN_HEADS_PER_STEP)
    assert G >= 2 and G & (G - 1) == 0
    nq = S // tq
    QB = _tile(nq, ATTN_QBLOCKS_PER_STEP)
    ns = nq // QB
    qk = q.shape[2]
    r = lax.broadcasted_iota(jnp.int32, (tk, tq), 0)
    c = lax.broadcasted_iota(jnp.int32, (tk, tq), 1)
    bias = jnp.stack([jnp.zeros((tk, tq), F32), jnp.where(c >= r, 0.0, -jnp.inf).astype(F32)])
    kern = functools.partial(_attn_kernel, tq=tq, tk=tk, G=G, QB=QB)
    return pl.pallas_call(
        kern,
        grid=(B, H // G, ns),
        in_specs=[
            pl.BlockSpec((1, G, qk, QB * tq), lambda b, h, i: (b, h, 0, i)),
            pl.BlockSpec((1, G, S, k.shape[-1]), lambda b, h, i: (b, h, 0, 0)),
            pl.BlockSpec((1, G, V_HEAD_DIM, S), lambda b, h, i: (b, h, 0, 0)),
            pl.BlockSpec((QB * tq, G * V_HEAD_DIM), lambda b, h, i: (b * ns + i, h)),
            pl.BlockSpec((2, tk, tq), lambda b, h, i: (0, 0, 0)),
        ],
        out_specs=pl.BlockSpec((QB * tq, G * V_HEAD_DIM), lambda b, h, i: (b * ns + i, h)),
        out_shape=jax.ShapeDtypeStruct((B * S, H * V_HEAD_DIM), BF16),
        scratch_shapes=[pltpu.VMEM((QB * G, 1, tq), F32), pltpu.VMEM((QB * G, 1, tq), F32),
                        pltpu.VMEM((QB * G, V_HEAD_DIM, tq), F32),
                        pltpu.VMEM((2, tk, tq), F32), pltpu.VMEM((2, tk, tq), BF16),
                        pltpu.VMEM((2, 1, tq), F32), pltpu.VMEM((4, 1, tq), F32)],
        compiler_params=_params("parallel", "parallel", "arbitrary"),
        name="attention",
    )(q, k, vt, sz, bias)


def _rope_tables(S):
    half = QK_ROPE_DIM // 2
    inv_freq = ROPE_BASE ** (-jnp.arange(half, dtype=F32) / half)
    ang = jnp.arange(S, dtype=jnp.int32).astype(F32)[:, None] * inv_freq[None, :]
    return jnp.cos(ang), jnp.sin(ang)


def kernel(x, c, w_ada, b_ada, ln_g, ln_b, a_w_in, a_w_dw, a_b_dw, a_norm_g, a_norm_b, a_w_out,
           b_w_in, b_q_norm_g, b_w_qb, b_w_out, kv_w_a, kv_norm_g, kv_w_b):
    B, S, D = x.shape
    depth = w_ada.shape[0]
    n_a = a_w_in.shape[0]
    alpha = (2.0 * depth) ** 0.25
    H = b_w_out.shape[1] // V_HEAD_DIM
    q_rank = b_q_norm_g.shape[1]
    kv_rank = kv_norm_g.shape[0]
    qk = QK_NOPE_DIM + QK_ROPE_DIM
    half = QK_ROPE_DIM // 2

    mod = _ada_mod(c, w_ada, b_ada)
    cos, sin = _rope_tables(S)
    k_tab = jnp.concatenate([cos, cos, -sin, sin], axis=1)

    x2 = x.reshape(B * S, D)
    kv = None
    h = None
    for layer in range(depth):
        if layer == n_a:
            rope_w = kv_w_a[:, kv_rank:]
            w_kva_ext = jnp.concatenate(
                [kv_w_a, rope_w[:, half:], rope_w[:, :half]], axis=1).astype(BF16)
            w_kvb = kv_w_b.reshape(kv_rank, H, QK_NOPE_DIM + V_HEAD_DIM)
            w_k = w_kvb[:, :, :QK_NOPE_DIM].reshape(kv_rank, H * QK_NOPE_DIM).astype(BF16)
            w_vt = w_kvb[:, :, QK_NOPE_DIM:].reshape(kv_rank, H * V_HEAD_DIM).T.astype(BF16)
            kv = _kv_proj(x2, w_kva_ext, kv_norm_g, k_tab, w_k, w_vt, B, S, H)
        if h is None:
            h = _ln_modulate(x2, mod[layer], S)
        if layer < n_a:
            i = layer
            v, sz = _glu_proj(h, a_w_in[i].astype(BF16))
            act = _conv_norm(v, sz, a_w_dw[i], a_b_dw[i], a_norm_g[i], a_norm_b[i], B, S)
            w_out = a_w_out[i]
        else:
            j = layer - n_a
            w_in = b_w_in[j]
            w_in = w_in.astype(BF16)
            q = _q_proj(h, w_in, b_q_norm_g[j], cos.T, sin.T,
                        b_w_qb[j].T.astype(BF16), B, S, H, qk ** -0.5 * math.log2(math.e))
            sz = _silu_proj(h, w_in, q_rank)
            act = _attention(q, kv[0], kv[1], sz, B, S, H)
            w_out = b_w_out[j]
        w_out = w_out.astype(BF16)
        if layer + 1 < depth:
            x2, h = _out_norm(act, w_out, x2, mod[layer], ln_g[layer], ln_b[layer], alpha, S,
                              mod_next=mod[layer + 1])
        else:
            x2 = _out_norm(act, w_out, x2, mod[layer], ln_g[layer], ln_b[layer], alpha, S)
    return x2.reshape(B, S, D)
```
